```python
import math
import jax, jax.numpy as jnp
from jax import lax
import numpy as np

D_MODEL = 1024
BATCH = 2
SEQ = 8192
DEPTH = 4
DEC_BATCH = 128
DEC_SEQ = 4
PAST_LEN = 2048
PAGE_SIZE = 128

MIX_WIDTH = D_MODEL
A_WIDTH = MIX_WIDTH // 2
A_CONV = 31
B_HEADS = 4
B_DK = 128
B_DV = 128
B_WIDTH = B_HEADS * B_DK
B_CHUNK = 64
C_HEADS = 8
C_HEAD_DIM = 128
C_WIDTH = C_HEADS * C_HEAD_DIM
IDX_HEADS = 8
IDX_DIM = 64
TOPK_MAX = 256
Q_BLOCK = 128
REL_BUCKETS = 32
REL_MAX_DIST = 128
D_FF = 2816
FFN_CONV = 3

N_EVEN = (DEPTH + 1) // 2
N_ODD = DEPTH // 2
EVEN_IN = 2 * A_WIDTH + 4 * B_WIDTH
ODD_IN = 3 * C_WIDTH + IDX_HEADS * IDX_DIM + IDX_DIM + IDX_HEADS
EPS = 1e-6

kernel_name = "hybrid_conv_hgrn2_dsa_decoder_step"

F32 = jnp.float32


def rmsnorm(x, g):
    xf = x.astype(F32)
    y = xf * lax.rsqrt(jnp.mean(xf * xf, axis=-1, keepdims=True) + EPS)
    return (y * g.astype(F32)).astype(x.dtype)


def layernorm(x, g, b):
    xf = x.astype(F32)
    mu = jnp.mean(xf, axis=-1, keepdims=True)
    var = jnp.mean(jnp.square(xf - mu), axis=-1, keepdims=True)
    return ((xf - mu) * lax.rsqrt(var + EPS) * g.astype(F32) + b.astype(F32)).astype(x.dtype)


def causal_dwconv(prefix, x, w, b):
    width = w.shape[0]
    xe = jnp.concatenate([prefix.astype(x.dtype), x], axis=1)
    y = lax.conv_general_dilated(xe, w[:, None, :].astype(x.dtype), window_strides=(1,), padding='VALID',
                                 dimension_numbers=('NWC', 'WIO', 'NWC'), feature_group_count=x.shape[-1])
    return y + b.astype(x.dtype), xe[:, xe.shape[1] - (width - 1):]


def t5_bucket(dist):
    n = jnp.maximum(dist, 0)
    max_exact = REL_BUCKETS // 2
    nf = jnp.maximum(n, 1).astype(F32)
    large = max_exact + (jnp.log(nf / max_exact) / math.log(REL_MAX_DIST / max_exact)
                         * (REL_BUCKETS - max_exact)).astype(jnp.int32)
    large = jnp.minimum(large, REL_BUCKETS - 1)
    return jnp.where(n < max_exact, n, large)


def gated_recurrence(q, k, v, logf, s0):
    bsz, L, H, _ = q.shape
    DV = v.shape[-1]
    c = min(B_CHUNK, L)
    pad = (-L) % c
    n = (L + pad) // c

    def prep(a):
        a = jnp.pad(a, ((0, 0), (0, pad), (0, 0), (0, 0)))
        return a.reshape(bsz, n, c, H, a.shape[-1]).transpose(1, 0, 3, 2, 4)

    tril = jnp.tril(jnp.ones((c, c), bool))

    def step(S, inp):
        qc, kc, vc, gc = inp
        b = jnp.cumsum(gc, axis=2)
        o_inter = jnp.einsum('bhtk,bhkv->bhtv', qc * jnp.exp(b), S)
        decay = jnp.exp(jnp.where(tril[:, :, None], b[:, :, :, None, :] - b[:, :, None, :, :], -jnp.inf))
        att = jnp.einsum('bhtk,bhsk,bhtsk->bhts', qc, kc, decay)
        o = o_inter + jnp.einsum('bhts,bhsv->bhtv', att, vc)
        b_last = b[:, :, -1, :]
        S = jnp.exp(b_last)[..., None] * S + jnp.einsum('bhsk,bhsv->bhkv', kc * jnp.exp(b_last[:, :, None, :] - b), vc)
        return S, o

    S, o = lax.scan(step, s0.astype(F32), (prep(q), prep(k), prep(v), prep(logf)))
    o = o.transpose(1, 0, 3, 2, 4).reshape(bsz, L + pad, H, DV)[:, :L]
    return o, S


def hgrn2(q_raw, f_raw, i_raw, g_raw, lb, gn, s0):
    bsz, L = q_raw.shape[:2]
    shp = (bsz, L, B_HEADS, B_DK)
    q = jax.nn.silu(q_raw.astype(F32)).reshape(shp)
    lbf = lb.astype(F32)
    logf = jnp.logaddexp(jnp.log(lbf), jnp.log1p(-lbf) + jax.nn.log_sigmoid(f_raw.astype(F32)))
    k = -jnp.expm1(logf)
    v = i_raw.astype(F32).reshape(bsz, L, B_HEADS, B_DV)
    o, S = gated_recurrence(q, k.reshape(shp), v, logf.reshape(shp), s0)
    o = o * lax.rsqrt(jnp.mean(o * o, axis=-1, keepdims=True) + EPS) * gn.astype(F32)
    o = o.reshape(bsz, L, B_WIDTH) * jax.nn.silu(g_raw.astype(F32))
    return o.astype(q_raw.dtype), S


def even_mixer(h, w_in, conv_w, conv_b, ln_g, ln_b, lb, gn, conv_prefix, s0):
    z = h @ w_in
    a_val, a_gate, bq, bf, bi, bg = jnp.split(
        z, [A_WIDTH, 2 * A_WIDTH, 2 * A_WIDTH + B_WIDTH, 2 * A_WIDTH + 2 * B_WIDTH, 2 * A_WIDTH + 3 * B_WIDTH], axis=-1)
    glu = a_val * jax.nn.sigmoid(a_gate)
    conv, new_prefix = causal_dwconv(conv_prefix, glu, conv_w, conv_b)
    a_out = jax.nn.silu(layernorm(conv, ln_g, ln_b))
    b_out, S = hgrn2(bq, bf, bi, bg, lb, gn, s0)
    return jnp.concatenate([a_out, b_out.astype(a_out.dtype)], axis=-1), new_prefix, S


def split_odd(z):
    q, k, v, qi, ki, wi = jnp.split(
        z, [C_WIDTH, 2 * C_WIDTH, 3 * C_WIDTH, 3 * C_WIDTH + IDX_HEADS * IDX_DIM,
            3 * C_WIDTH + IDX_HEADS * IDX_DIM + IDX_DIM], axis=-1)
    bsz, L = z.shape[:2]
    return (q.reshape(bsz, L, C_HEADS, C_HEAD_DIM), k.reshape(bsz, L, C_HEADS, C_HEAD_DIM),
            v.reshape(bsz, L, C_HEADS, C_HEAD_DIM), qi.reshape(bsz, L, IDX_HEADS, IDX_DIM), ki, wi)


def indexer_topk(qi, wi, ki, q_pos, top_k):
    dots = jnp.einsum('bthd,bsd->bths', qi.astype(F32), ki.astype(F32)) * (IDX_DIM ** -0.5)
    score = jnp.einsum('bths,bth->bts', jax.nn.relu(dots), wi.astype(F32) * (IDX_HEADS ** -0.5))
    k_pos = jnp.arange(ki.shape[1], dtype=jnp.int32)
    score = jnp.where((k_pos[None, :] <= q_pos[:, None])[None], score, -jnp.inf)
    _, idx = lax.top_k(score, top_k)
    valid = idx <= q_pos[None, :, None]
    return idx, valid


def dsa_prompt(q, k, v, qi, ki, wi, rel_bias):
    bsz, S = q.shape[:2]
    top_k = min(TOPK_MAX, S // 4)
    qb = min(Q_BLOCK, S)
    nb = S // qb

    def blocks(a):
        return jnp.moveaxis(a.reshape((bsz, nb, qb) + a.shape[2:]), 1, 0)

    pos_blocks = jnp.arange(S, dtype=jnp.int32).reshape(nb, qb)

    def one_block(args):
        q_b, qi_b, wi_b, pos_b = args
        idx, valid = indexer_topk(qi_b, wi_b, ki, pos_b, top_k)
        k_sel = jax.vmap(lambda kk, ii: kk[ii])(k, idx)
        v_sel = jax.vmap(lambda vv, ii: vv[ii])(v, idx)
        logits = jnp.einsum('bthd,btkhd->bthk', q_b, k_sel, preferred_element_type=F32) * (C_HEAD_DIM ** -0.5)
        bias = rel_bias[t5_bucket(pos_b[None, :, None] - idx)].astype(F32)
        logits = jnp.where(valid[:, :, None, :], logits + jnp.moveaxis(bias, -1, 2), -jnp.inf)
        p = jax.nn.softmax(logits, axis=-1)
        return jnp.einsum('bthk,btkhd->bthd', p.astype(v.dtype), v_sel)

    out = lax.map(one_block, (blocks(q), blocks(qi), blocks(wi), pos_blocks))
    return jnp.moveaxis(out, 0, 1).reshape(bsz, S, C_WIDTH)


def dsa_sample(q, k_new, v_new, qi, ki_new, wi, pool_k, pool_v, pool_ki, page_table, rel_bias):
    db, T = q.shape[:2]
    past = page_table.shape[1] * PAGE_SIZE
    top_k = min(TOPK_MAX, (past + T) // 4)
    ki_past = pool_ki[page_table].reshape(db, past, IDX_DIM)
    ki_all = jnp.concatenate([ki_past, ki_new.astype(ki_past.dtype)], axis=1)
    q_pos = past + jnp.arange(T, dtype=jnp.int32)
    idx, valid = indexer_topk(qi, wi, ki_all, q_pos, top_k)
    old = jnp.minimum(idx, past - 1)
    phys = jax.vmap(lambda pt, i: pt[i // PAGE_SIZE])(page_table, old)
    off = old % PAGE_SIZE
    k_old = pool_k[phys, off]
    v_old = pool_v[phys, off]
    valid_old = valid & (idx < past)
    sel_new = jnp.any(idx[:, :, :, None] == q_pos[None, None, None, :], axis=2)
    sel_new = sel_new & (q_pos[None, :] <= q_pos[:, None])[None]
    scale = C_HEAD_DIM ** -0.5
    lg_old = jnp.einsum('bthd,btkhd->bthk', q, k_old, preferred_element_type=F32) * scale
    lg_old = lg_old + jnp.moveaxis(rel_bias[t5_bucket(q_pos[None, :, None] - idx)].astype(F32), -1, 2)
    lg_old = jnp.where(valid_old[:, :, None, :], lg_old, -jnp.inf)
    lg_new = jnp.einsum('bthd,bjhd->bthj', q, k_new, preferred_element_type=F32) * scale
    lg_new = lg_new + jnp.transpose(rel_bias[t5_bucket(q_pos[:, None] - q_pos[None, :])].astype(F32), (0, 2, 1))[None]
    lg_new = jnp.where(sel_new[:, :, None, :], lg_new, -jnp.inf)
    p = jax.nn.softmax(jnp.concatenate([lg_old, lg_new], axis=-1), axis=-1)
    p_old, p_new = p[..., :top_k], p[..., top_k:]
    out = (jnp.einsum('bthk,btkhd->bthd', p_old.astype(v_old.dtype), v_old)
           + jnp.einsum('bthj,bjhd->bthd', p_new.astype(v_new.dtype), v_new))
    return out.reshape(db, T, C_WIDTH).astype(q.dtype)


def conv_ffn(h, w_up, conv_w, conv_b, w_down, prefix):
    u = h @ w_up
    uc, new_prefix = causal_dwconv(prefix, u, conv_w, conv_b)
    gate, val = jnp.split(uc, 2, axis=-1)
    return (jax.nn.silu(gate) * val) @ w_down, new_prefix


def setup_inputs(seed: int = 0) -> dict:
    key = jax.random.key(seed)
    ks = jax.random.split(key, 28)
    n_pages = PAST_LEN // PAGE_SIZE
    n_used = DEC_BATCH * n_pages
    n_phys = n_used + max(1, n_used // 4)

    def nrm(k, shape, scale):
        return jax.random.normal(k, shape, F32) * scale

    return {
        "x_prompt": nrm(ks[0], (BATCH, SEQ, D_MODEL), 1.0),
        "x_sample": nrm(ks[1], (DEC_BATCH, DEC_SEQ, D_MODEL), 1.0),
        "state_conv_a": nrm(ks[2], (N_EVEN, DEC_BATCH, A_CONV - 1, A_WIDTH), 0.5),
        "state_hgrn": nrm(ks[3], (N_EVEN, DEC_BATCH, B_HEADS, B_DK, B_DV), 0.1),
        "cache_k": nrm(ks[4], (N_ODD, n_phys, PAGE_SIZE, C_HEADS, C_HEAD_DIM), 1.0),
        "cache_v": nrm(ks[5], (N_ODD, n_phys, PAGE_SIZE, C_HEADS, C_HEAD_DIM), 1.0),
        "cache_ki": nrm(ks[6], (N_ODD, n_phys, PAGE_SIZE, IDX_DIM), 1.0),
        "state_ffn_conv": nrm(ks[7], (DEPTH, DEC_BATCH, FFN_CONV - 1, 2 * D_FF), 1.0),
        "page_table": jax.random.permutation(ks[8], n_phys)[:n_used].reshape(DEC_BATCH, n_pages).astype(jnp.int32),
        "norm_mix": 1.0 + nrm(ks[9], (DEPTH, D_MODEL), 0.02),
        "norm_ffn": 1.0 + nrm(ks[10], (DEPTH, D_MODEL), 0.02),
        "norm_final": 1.0 + nrm(ks[11], (D_MODEL,), 0.02),
        "w_in_even": nrm(ks[12], (N_EVEN, D_MODEL, EVEN_IN), D_MODEL ** -0.5),
        "w_in_odd": nrm(ks[13], (N_ODD, D_MODEL, ODD_IN), D_MODEL ** -0.5),
        "w_out": nrm(ks[14], (DEPTH, MIX_WIDTH, D_MODEL), MIX_WIDTH ** -0.5),
        "conv_a_w": nrm(ks[15], (N_EVEN, A_CONV, A_WIDTH), A_CONV ** -0.5),
        "conv_a_b": nrm(ks[16], (N_EVEN, A_WIDTH), 0.02),
        "ln_a_g": 1.0 + nrm(ks[17], (N_EVEN, A_WIDTH), 0.02),
        "ln_a_b": nrm(ks[18], (N_EVEN, A_WIDTH), 0.02),
        "hgrn_lb": nrm(ks[19], (N_EVEN, B_WIDTH), 0.5),
        "hgrn_gn": 1.0 + nrm(ks[20], (N_EVEN, B_DV), 0.02),
        "rel_bias": nrm(ks[21], (REL_BUCKETS, C_HEADS), 0.5),
        "w_up": nrm(ks[22], (DEPTH, D_MODEL, 2 * D_FF), D_MODEL ** -0.5),
        "ffn_conv_w": nrm(ks[23], (DEPTH, FFN_CONV, 2 * D_FF), FFN_CONV ** -0.5),
        "ffn_conv_b": nrm(ks[24], (DEPTH, 2 * D_FF), 0.02),
        "w_down": nrm(ks[25], (DEPTH, D_FF, D_MODEL), D_FF ** -0.5),
    }


def reference(x_prompt, x_sample, state_conv_a, state_hgrn, cache_k, cache_v, cache_ki, state_ffn_conv,
              page_table, norm_mix, norm_ffn, norm_final, w_in_even, w_in_odd, w_out, conv_a_w, conv_a_b,
              ln_a_g, ln_a_b, hgrn_lb, hgrn_gn, rel_bias, w_up, ffn_conv_w, ffn_conv_b, w_down):
    xp, xs = x_prompt, x_sample
    bsz = xp.shape[0]
    lbs = jnp.cumsum(jax.nn.softmax(hgrn_lb.astype(F32), axis=0), axis=0)
    lbs = lbs - lbs[0:1]
    p_conv, p_hg, p_k, p_v, p_ki, p_ffn = [], [], [], [], [], []
    s_conv, s_hg, s_k, s_v, s_ki, s_ffn = [], [], [], [], [], []
    for l in range(DEPTH):
        j = l // 2
        hp = rmsnorm(xp, norm_mix[l])
        hs = rmsnorm(xs, norm_mix[l])
        if l % 2 == 0:
            zero_conv = jnp.zeros((bsz, A_CONV - 1, A_WIDTH), xp.dtype)
            zero_s = jnp.zeros((bsz, B_HEADS, B_DK, B_DV), F32)
            mp, cp, sp = even_mixer(hp, w_in_even[j], conv_a_w[j], conv_a_b[j], ln_a_g[j], ln_a_b[j],
                                    lbs[j], hgrn_gn[j], zero_conv, zero_s)
            ms, cs, ss = even_mixer(hs, w_in_even[j], conv_a_w[j], conv_a_b[j], ln_a_g[j], ln_a_b[j],
                                    lbs[j], hgrn_gn[j], state_conv_a[j], state_hgrn[j])
            p_conv.append(cp.astype(state_conv_a.dtype))
            p_hg.append(sp.astype(state_hgrn.dtype))
            s_conv.append(cs.astype(state_conv_a.dtype))
            s_hg.append(ss.astype(state_hgrn.dtype))
        else:
            qp, kp, vp, qip, kip, wip = split_odd(hp @ w_in_odd[j])
            mp = dsa_prompt(qp, kp, vp, qip, kip, wip, rel_bias)
            qs, ks_, vs, qis, kis, wis = split_odd(hs @ w_in_odd[j])
            ms = dsa_sample(qs, ks_, vs, qis, kis, wis, cache_k[j], cache_v[j], cache_ki[j], page_table, rel_bias)
            p_k.append(kp.astype(cache_k.dtype))
            p_v.append(vp.astype(cache_v.dtype))
            p_ki.append(kip.astype(cache_ki.dtype))
            s_k.append(ks_.astype(cache_k.dtype))
            s_v.append(vs.astype(cache_v.dtype))
            s_ki.append(kis.astype(cache_ki.dtype))
        xp = xp + (mp @ w_out[l]).astype(xp.dtype)
        xs = xs + (ms @ w_out[l]).astype(xs.dtype)
        zero_ffn = jnp.zeros((bsz, FFN_CONV - 1, 2 * D_FF), xp.dtype)
        fp, fcp = conv_ffn(rmsnorm(xp, norm_ffn[l]), w_up[l], ffn_conv_w[l], ffn_conv_b[l], w_down[l], zero_ffn)
        fs, fcs = conv_ffn(rmsnorm(xs, norm_ffn[l]), w_up[l], ffn_conv_w[l], ffn_conv_b[l], w_down[l], state_ffn_conv[l])
        xp = xp + fp.astype(xp.dtype)
        xs = xs + fs.astype(xs.dtype)
        p_ffn.append(fcp.astype(state_ffn_conv.dtype))
        s_ffn.append(fcs.astype(state_ffn_conv.dtype))
    y_prompt = rmsnorm(xp, norm_final)
    y_sample = rmsnorm(xs, norm_final)
    return (y_prompt, y_sample,
            jnp.stack(p_conv), jnp.stack(p_hg), jnp.stack(p_k), jnp.stack(p_v), jnp.stack(p_ki), jnp.stack(p_ffn),
            jnp.stack(s_conv), jnp.stack(s_hg), jnp.stack(s_k), jnp.stack(s_v), jnp.stack(s_ki), jnp.stack(s_ffn))
```

```python
import functools
import math

import jax
import jax.numpy as jnp
from jax import lax
from jax.experimental import pallas as pl
from jax.experimental.pallas import tpu as pltpu

F32 = jnp.float32
BF16 = jnp.bfloat16
I32 = jnp.int32

EPS = 1e-6
D_MODEL = 1024
A_WIDTH = 512
A_CONV = 31
B_HEADS = 4
B_DK = 128
B_WIDTH = 512
C_HEADS = 8
C_HEAD_DIM = 128
C_WIDTH = 1024
IDX_HEADS = 8
IDX_DIM = 64
TOPK = 256
REL_BUCKETS = 32
REL_MAX_DIST = 128
D_FF = 2816
PAGE = 128

LANES = 128
VMEM_LIMIT = 60 * 1024 * 1024
NEG_BIG = -1e30
INT_MIN = -2 ** 31


def _cparams(sem):
    return pltpu.CompilerParams(dimension_semantics=sem, vmem_limit_bytes=VMEM_LIMIT)


def _rms(x, g):
    return x * lax.rsqrt(jnp.mean(x * x, axis=-1, keepdims=True) + EPS) * g


def _sigmoid(x):
    return 1.0 / (1.0 + jnp.exp(-x))


def _dot(a, b):
    return jnp.dot(a, b, preferred_element_type=F32)


def _dot_nt(a, b):
    return lax.dot_general(a, b, (((1,), (1,)), ((), ())), preferred_element_type=F32)


def _norm_matmul_kernel(x_ref, g_ref, w_ref, *out_refs, segs):
    hb = _rms(x_ref[...], g_ref[...]).astype(BF16)
    for start, width, scale, outs in segs:
        for c0 in range(0, width, 512):
            cw = min(512, width - c0)
            r = _dot(hb, w_ref[:, start + c0:start + c0 + cw])
            if scale != 1.0:
                r = r * scale
            for oi in outs:
                out_refs[oi][:, c0:c0 + cw] = r.astype(out_refs[oi].dtype)


def norm_matmul(x, g, w16, segs, out_defs, tm=512):
    m, d = x.shape
    tm = min(tm, m)
    n = w16.shape[1]
    return pl.pallas_call(
        functools.partial(_norm_matmul_kernel, segs=segs),
        grid=(m // tm,),
        in_specs=[pl.BlockSpec((tm, d), lambda i: (i, 0)),
                  pl.BlockSpec((1, d), lambda i: (0, 0)),
                  pl.BlockSpec((d, n), lambda i: (0, 0))],
        out_specs=[pl.BlockSpec((tm, wd), lambda i: (i, 0)) for wd, _ in out_defs],
        out_shape=[jax.ShapeDtypeStruct((m, wd), dt) for wd, dt in out_defs],
        compiler_params=_cparams(("arbitrary",)),
        name="norm_matmul",
    )(x, g.reshape(1, d), w16)


def _outproj_kernel(*refs, n_in):
    x_ref = refs[0]
    o_ref = refs[-1]
    acc = x_ref[...]
    for i in range(n_in):
        acc = acc + _dot(refs[1 + 2 * i][...].astype(BF16), refs[2 + 2 * i][...])
    o_ref[...] = acc


def outproj(x, ms, ws, tm=512):
    m, d = x.shape
    tm = min(tm, m)
    in_specs = [pl.BlockSpec((tm, d), lambda i: (i, 0))]
    args = [x]
    for mm, w in zip(ms, ws):
        in_specs.append(pl.BlockSpec((tm, mm.shape[1]), lambda i: (i, 0)))
        in_specs.append(pl.BlockSpec(w.shape, lambda i: (0, 0)))
        args += [mm, w]
    return pl.pallas_call(
        functools.partial(_outproj_kernel, n_in=len(ms)),
        grid=(m // tm,),
        in_specs=in_specs,
        out_specs=pl.BlockSpec((tm, d), lambda i: (i, 0)),
        out_shape=jax.ShapeDtypeStruct((m, d), F32),
        compiler_params=_cparams(("arbitrary",)),
        name="outproj",
    )(*args)


def _ffn_kernel(x_ref, g_ref, wup_ref, cw_ref, cb_ref, wdn_ref, pre_ref, o_ref, npre_ref,
                h_s, acc_s, ubuf_s, carry_s, *, tm, fc, dil, pad, tiles_per_seq):
    i = pl.program_id(0)
    c = pl.program_id(1)
    nc = pl.num_programs(1)

    @pl.when(c == 0)
    def _():
        h_s[...] = _rms(x_ref[...], g_ref[...]).astype(BF16)

    u = _dot(h_s[...], wup_ref[...])
    ubuf_s[pad:pad + tm, :] = u
    first = (i % tiles_per_seq) == 0

    @pl.when(first)
    def _():
        ubuf_s[pad - 2 * dil:pad, :] = pre_ref[0]

    @pl.when(jnp.logical_not(first))
    def _():
        ubuf_s[pad - 2 * dil:pad, :] = carry_s[c]

    tail = ubuf_s[pad + tm - 2 * dil:pad + tm, :]
    carry_s[c] = tail
    npre_ref[0] = tail
    cw = cw_ref[0]
    uc = (u * cw[2:3, :] + ubuf_s[pad - dil:pad - dil + tm, :] * cw[1:2, :]
          + ubuf_s[pad - 2 * dil:pad - 2 * dil + tm, :] * cw[0:1, :] + cb_ref[0])
    gate = uc[:, :fc]
    val = uc[:, fc:]
    act = (gate * _sigmoid(gate) * val).astype(BF16)
    part = _dot(act, wdn_ref[...])

    @pl.when(c == 0)
    def _():
        acc_s[...] = part

    @pl.when(c > 0)
    def _():
        acc_s[...] += part

    @pl.when(c == nc - 1)
    def _():
        o_ref[...] = x_ref[...] + acc_s[...]


def arrange_cols(a, fc):
    nc = D_FF // fc
    lead = a.shape[:-1]
    return a.reshape(lead + (2, nc, fc)).swapaxes(-3, -2).reshape(lead + (nc * 2 * fc,))


def unarrange_cols(a, fc):
    nc = D_FF // fc
    lead = a.shape[:-1]
    return a.reshape(lead + (nc, 2, fc)).swapaxes(-3, -2).reshape(lead + (2 * D_FF,))


def prep_ffn_weights(w_up, cw, cb, w_dn, fc):
    nc = D_FF // fc
    wup16 = arrange_cols(w_up, fc).astype(BF16)
    cwr = arrange_cols(cw, fc).reshape(cw.shape[0], nc, 2 * fc).swapaxes(0, 1)
    cbr = arrange_cols(cb, fc).reshape(nc, 1, 2 * fc)
    return wup16, cwr, cbr, w_dn.astype(BF16)


def ffn(x, g, wup16, cw, cb, wdn16, pre, *, n_seq, dil, tm, fc=256):
    m, d = x.shape
    nc = D_FF // fc
    tiles_per_seq = (m // n_seq) // tm
    pad = max(8, 2 * dil)
    kern = functools.partial(_ffn_kernel, tm=tm, fc=fc, dil=dil, pad=pad, tiles_per_seq=tiles_per_seq)
    return pl.pallas_call(
        kern,
        grid=(m // tm, nc),
        in_specs=[pl.BlockSpec((tm, d), lambda i, c: (i, 0)),
                  pl.BlockSpec((1, d), lambda i, c: (0, 0)),
                  pl.BlockSpec((d, 2 * fc), lambda i, c: (0, c)),
                  pl.BlockSpec((1, 3, 2 * fc), lambda i, c: (c, 0, 0)),
                  pl.BlockSpec((1, 1, 2 * fc), lambda i, c: (c, 0, 0)),
                  pl.BlockSpec((fc, d), lambda i, c: (c, 0)),
                  pl.BlockSpec((1, 2 * dil, 2 * fc), lambda i, c: (i // tiles_per_seq, 0, c))],
        out_specs=[pl.BlockSpec((tm, d), lambda i, c: (i, 0)),
                   pl.BlockSpec((1, 2 * dil, 2 * fc), lambda i, c: (i, 0, c))],
        out_shape=[jax.ShapeDtypeStruct((m, d), F32),
                   jax.ShapeDtypeStruct((m // tm, 2 * dil, nc * 2 * fc), F32)],
        scratch_shapes=[pltpu.VMEM((tm, d), BF16),
                        pltpu.VMEM((tm, d), F32),
                        pltpu.VMEM((pad + tm, 2 * fc), F32),
                        pltpu.VMEM((nc, 2 * dil, 2 * fc), F32)],
        compiler_params=_cparams(("arbitrary", "arbitrary")),
        name="conv_ffn",
    )(x, g.reshape(1, d), wup16, cw, cb, wdn16, pre)


def _rmsnorm_kernel(x_ref, g_ref, o_ref):
    o_ref[...] = _rms(x_ref[...], g_ref[...])


def rmsnorm_call(x, g, tm=512):
    m, d = x.shape
    tm = min(tm, m)
    return pl.pallas_call(
        _rmsnorm_kernel,
        grid=(m // tm,),
        in_specs=[pl.BlockSpec((tm, d), lambda i: (i, 0)), pl.BlockSpec((1, d), lambda i: (0, 0))],
        out_specs=pl.BlockSpec((tm, d), lambda i: (i, 0)),
        out_shape=jax.ShapeDtypeStruct((m, d), F32),
        compiler_params=_cparams(("arbitrary",)),
        name="final_rmsnorm",
    )(x, g.reshape(1, d))


def _mixer_a_kernel(val_ref, gate_ref, cw_ref, cb_ref, lg_ref, lb_ref, pre_ref, o_ref, npre_ref,
                    gbuf_s, *, tm, dil, pad, tiles_per_seq):
    i = pl.program_id(0)
    first = (i % tiles_per_seq) == 0
    hist = (A_CONV - 1) * dil

    @pl.when(first)
    def _():
        gbuf_s[pad - hist:pad, :] = pre_ref[0]

    glu = val_ref[...] * _sigmoid(gate_ref[...])
    gbuf_s[pad:pad + tm, :] = glu
    npre_ref[0] = gbuf_s[pad + tm - hist:pad + tm, :]
    acc = jnp.zeros((tm, A_WIDTH), F32) + cb_ref[...]
    for j in range(A_CONV):
        s0 = pad - hist + j * dil
        acc = acc + gbuf_s[s0:s0 + tm, :] * cw_ref[j:j + 1, :]
    mu = jnp.mean(acc, axis=-1, keepdims=True)
    xc = acc - mu
    var = jnp.mean(xc * xc, axis=-1, keepdims=True)
    y = xc * lax.rsqrt(var + EPS) * lg_ref[...] + lb_ref[...]
    o_ref[...] = (y * _sigmoid(y)).astype(o_ref.dtype)
    if tiles_per_seq > 1:
        gbuf_s[0:pad, :] = gbuf_s[tm:tm + pad, :]


def mixer_a(z, cw, cb, lg, lb, pre, *, n_seq, dil, tm):
    m = z.shape[0]
    tiles_per_seq = (m // n_seq) // tm
    hist = (A_CONV - 1) * dil
    pad = 32 if dil == 1 else hist
    kern = functools.partial(_mixer_a_kernel, tm=tm, dil=dil, pad=pad, tiles_per_seq=tiles_per_seq)
    vec = lambda a: a.reshape(1, A_WIDTH)
    return pl.pallas_call(
        kern,
        grid=(m // tm,),
        in_specs=[pl.BlockSpec((tm, A_WIDTH), lambda i: (i, 0)),
                  pl.BlockSpec((tm, A_WIDTH), lambda i: (i, 1)),
                  pl.BlockSpec((A_CONV, A_WIDTH), lambda i: (0, 0)),
                  pl.BlockSpec((1, A_WIDTH), lambda i: (0, 0)),
                  pl.BlockSpec((1, A_WIDTH), lambda i: (0, 0)),
                  pl.BlockSpec((1, A_WIDTH), lambda i: (0, 0)),
                  pl.BlockSpec((1, hist, A_WIDTH), lambda i: (i // tiles_per_seq, 0, 0))],
        out_specs=[pl.BlockSpec((tm, A_WIDTH), lambda i: (i, 0)),
                   pl.BlockSpec((1, hist, A_WIDTH), lambda i: (i // tiles_per_seq, 0, 0))],
        out_shape=[jax.ShapeDtypeStruct((m, A_WIDTH), BF16),
                   jax.ShapeDtypeStruct((n_seq, hist, A_WIDTH), F32)],
        scratch_shapes=[pltpu.VMEM((pad + tm, A_WIDTH), F32)],
        compiler_params=_cparams(("arbitrary",)),
        name="mixer_a",
    )(z, z, cw, vec(cb), vec(lg), vec(lb), pre)


HG_CHUNK = 128
HG_SUB = 16


def _split3(x):
    hi = x.astype(BF16)
    r1 = x - hi.astype(F32)
    mid = r1.astype(BF16)
    lo = (r1 - mid.astype(F32)).astype(BF16)
    return hi, mid, lo


def _hgrn_chunk(qr, fr, ir, gr, llb, l1m, gn, st_s, b_s, q_s, k_s, v_s, oacc_s, *, n_real, nsb):
    c = HG_CHUNK
    q = qr * _sigmoid(qr)
    ls = jnp.minimum(fr, 0.0) - jnp.log1p(jnp.exp(-jnp.abs(fr)))
    bt = l1m + ls
    logf = jnp.maximum(llb, bt) + jnp.log1p(jnp.exp(-jnp.abs(llb - bt)))
    kk = 1.0 - jnp.exp(logf)
    if n_real < c:
        real = lax.broadcasted_iota(I32, (c, B_WIDTH), 0) < n_real
        logf = jnp.where(real, logf, 0.0)
        kk = jnp.where(real, kk, 0.0)
    ri = lax.broadcasted_iota(I32, (c, c), 0)
    ci = lax.broadcasted_iota(I32, (c, c), 1)
    tril = jnp.where(ci <= ri, 1.0, 0.0).astype(BF16)
    hi, mid, lo = _split3(logf)
    b = _dot(tril, hi) + _dot(tril, mid) + _dot(tril, lo)
    b_s[...] = b
    q_s[...] = q
    k_s[...] = kk
    v_s[...] = ir
    sub = HG_SUB
    row16 = lax.broadcasted_iota(I32, (sub, 1), 0)
    outs = []
    for h in range(B_HEADS):
        sl = slice(h * B_DK, (h + 1) * B_DK)
        st = st_s[h]
        oacc_s[:, sl] = _dot_nt((q[:, sl] * jnp.exp(b[:, sl])).astype(BF16), st.astype(BF16))
        for i in range(nsb):
            r0 = i * sub
            qb = q_s[r0:r0 + sub, sl]
            bb = b_s[r0:r0 + sub, sl]
            o_blk = jnp.zeros((sub, B_DK), F32)
            if i > 0:
                ref_b = b_s[r0 - 1:r0, sl]
                qt = (qb * jnp.exp(bb - ref_b)).astype(BF16)
                kt = (k_s[0:r0, sl] * jnp.exp(ref_b - b_s[0:r0, sl])).astype(BF16)
                att = _dot_nt(qt, kt)
                o_blk = o_blk + _dot(att.astype(BF16), v_s[0:r0, sl].astype(BF16))
            for s in range(sub):
                r = r0 + s
                if r >= n_real:
                    continue
                p = qb * jnp.exp(bb - b_s[r:r + 1, sl]) * k_s[r:r + 1, sl]
                a = jnp.sum(p, axis=1, keepdims=True)
                a = jnp.where(row16 >= s, a, 0.0)
                o_blk = o_blk + a * v_s[r:r + 1, sl]
            oacc_s[r0:r0 + sub, sl] += o_blk
        o = oacc_s[:, sl]
        on = o * lax.rsqrt(jnp.mean(o * o, axis=-1, keepdims=True) + EPS) * gn
        g = gr[:, sl]
        outs.append(on * (g * _sigmoid(g)))
        bl = b_s[c - 1:c, sl]
        kd = (k_s[:, sl] * jnp.exp(bl - b_s[:, sl])).astype(BF16)
        vt = v_s[:, sl].T.astype(BF16)
        st_s[h] = st * jnp.exp(bl) + _dot(vt, kd)
    return jnp.concatenate(outs, axis=1)


def _hgrn_kernel(q_ref, f_ref, i_ref, g_ref, llb_ref, l1m_ref, gn_ref, s0_ref, o_ref, sout_ref,
                 st_s, b_s, q_s, k_s, v_s, oacc_s, *, rt, n_real, nsb, tiles_per_seq):
    c = HG_CHUNK
    first = (pl.program_id(0) % tiles_per_seq) == 0

    @pl.when(first)
    def _():
        for h in range(B_HEADS):
            st_s[h] = s0_ref[0, h].T

    rows = min(rt, c)
    for ch in range(max(1, rt // c)):
        rs = slice(ch * c, ch * c + rows)

        def load(ref):
            x = ref[rs, :]
            if rows < c:
                x = jnp.concatenate([x, jnp.zeros((c - rows, B_WIDTH), F32)], axis=0)
            return x

        out = _hgrn_chunk(load(q_ref), load(f_ref), load(i_ref), load(g_ref),
                          llb_ref[...], l1m_ref[...], gn_ref[...],
                          st_s, b_s, q_s, k_s, v_s, oacc_s, n_real=n_real, nsb=nsb)
        o_ref[rs, :] = out[0:rows, :].astype(o_ref.dtype)
    for h in range(B_HEADS):
        sout_ref[0, h] = st_s[h].T


def hgrn(z, col0, llb, l1m, gn, s0, *, n_seq, rt, n_real, nsb):
    m = z.shape[0]
    tiles_per_seq = (m // n_seq) // rt
    kern = functools.partial(_hgrn_kernel, rt=rt, n_real=n_real, nsb=nsb, tiles_per_seq=tiles_per_seq)
    c = HG_CHUNK
    zspec = lambda k: pl.BlockSpec((rt, B_WIDTH), lambda i: (i, col0 + k))
    state_spec = pl.BlockSpec((1, B_HEADS, B_DK, B_DK), lambda i: (i // tiles_per_seq, 0, 0, 0))
    return pl.pallas_call(
        kern,
        grid=(m // rt,),
        in_specs=[zspec(0), zspec(1), zspec(2), zspec(3),
                  pl.BlockSpec((1, B_WIDTH), lambda i: (0, 0)),
                  pl.BlockSpec((1, B_WIDTH), lambda i: (0, 0)),
                  pl.BlockSpec((1, B_DK), lambda i: (0, 0)),
                  state_spec],
        out_specs=[pl.BlockSpec((rt, B_WIDTH), lambda i: (i, 0)), state_spec],
        out_shape=[jax.ShapeDtypeStruct((m, B_WIDTH), BF16),
                   jax.ShapeDtypeStruct((n_seq, B_HEADS, B_DK, B_DK), F32)],
        scratch_shapes=[pltpu.VMEM((B_HEADS, B_DK, B_DK), F32),
                        pltpu.VMEM((c, B_WIDTH), F32),
                        pltpu.VMEM((c, B_WIDTH), F32),
                        pltpu.VMEM((c, B_WIDTH), F32),
                        pltpu.VMEM((c, B_WIDTH), F32),
                        pltpu.VMEM((c, B_WIDTH), F32)],
        compiler_params=_cparams(("arbitrary",)),
        name="hgrn2",
    )(z, z, z, z, llb.reshape(1, B_WIDTH), l1m.reshape(1, B_WIDTH), gn.reshape(1, B_DK), s0)


def _order_key(x):
    b = lax.bitcast_convert_type(x + 0.0, I32)
    return jnp.where(b < 0, b ^ 0x7FFFFFFF, b)


def _t5_bucket(dist):
    n = jnp.maximum(dist, 0)
    max_exact = REL_BUCKETS // 2
    nf = jnp.maximum(n, 1).astype(F32)
    large = max_exact + (jnp.log(nf / max_exact) / math.log(REL_MAX_DIST / max_exact)
                         * (REL_BUCKETS - max_exact)).astype(I32)
    large = jnp.minimum(large, REL_BUCKETS - 1)
    return jnp.where(n < max_exact, n, large)


def _bias_table(dist, rb_ref, h):
    bucket = _t5_bucket(dist)
    last = rb_ref[REL_BUCKETS - 1, h]
    out = jnp.zeros(dist.shape, F32)
    for bk in range(REL_BUCKETS - 1):
        out = jnp.where(bucket == bk, rb_ref[bk, h] - last, out)
    return out


def _wide(x, n):
    return jnp.concatenate([x] * (n // LANES), axis=1)


def _dsa_prompt_kernel(q_ref, qi_ref, s_ref, k_ref, v_ref, kit_ref, rb_ref, o_ref,
                       keys_s, wb_s, bias_s, m_s, l_s, acc_s, j_s, *, tq, nbits):
    b = pl.program_id(0)
    i = pl.program_id(1)
    row = lax.broadcasted_iota(I32, (tq, tq), 0)
    col = lax.broadcasted_iota(I32, (tq, tq), 1)

    @pl.when((b == 0) & (i == 0))
    def _():
        for h in range(C_HEADS):
            bias_s[0, h] = _bias_table(row + tq - col, rb_ref, h)
            bias_s[1, h] = _bias_table(row - col, rb_ref, h)

    wscale = IDX_DIM ** -0.5 * IDX_HEADS ** -0.5
    for h in range(IDX_HEADS):
        wcol = s_ref[:, IDX_DIM + h:IDX_DIM + h + 1] * wscale
        wb_s[h] = jnp.broadcast_to(wcol, (tq, LANES))

    def score_tile(j, carry):
        off = pl.multiple_of(j * tq, tq)
        kt = kit_ref[0, :, pl.ds(off, tq)]
        sc = jnp.zeros((tq, tq), F32)
        for h in range(IDX_HEADS):
            d = _dot(qi_ref[:, h * LANES:(h + 1) * LANES], kt)
            sc = sc + jnp.maximum(d, 0.0) * _wide(wb_s[h], tq)
        sc = jnp.where((j == i) & (col > row), -jnp.inf, sc)
        keys_s[:, pl.ds(off, tq)] = _order_key(sc)
        return carry

    lax.fori_loop(0, i + 1, score_tile, 0)

    def count(pred_fn):
        def body(j, acc):
            off = pl.multiple_of(j * tq, tq)
            kt = keys_s[:, pl.ds(off, tq)]
            for hf in range(tq // LANES):
                acc = acc + jnp.where(pred_fn(kt[:, hf * LANES:(hf + 1) * LANES], off + hf * LANES), 1, 0)
            return acc
        acc = lax.fori_loop(0, i + 1, body, jnp.zeros((tq, LANES), I32))
        return jnp.sum(acc, axis=1, keepdims=True)

    def bit_step(bi, thr):
        cand = thr + lax.shift_left(jnp.int32(1), 31 - bi)
        cnt = count(lambda kt, off: kt >= cand)
        return jnp.where(cnt >= TOPK, cand, thr)

    thr = lax.fori_loop(0, 32, bit_step, jnp.full((tq, LANES), INT_MIN, I32))

    n_ge = count(lambda kt, off: kt >= thr)
    n_gt = count(lambda kt, off: kt > thr)
    need = TOPK - n_gt
    lane = lax.broadcasted_iota(I32, (tq, LANES), 1)
    j_s[...] = jnp.full((tq, LANES), 2 ** 30, I32)
    any_tie = jnp.max(jnp.where(n_ge > TOPK, 1.0, 0.0)) > 0.0

    @pl.when(any_tie)
    def _():
        def idx_step(bi, jv):
            cand = jv + lax.shift_left(jnp.int32(1), nbits - 1 - bi)
            cnt = count(lambda kt, off: (kt == thr) & (lane + off < cand))
            return jnp.where(cnt <= need - 1, cand, jv)
        j_s[...] = lax.fori_loop(0, nbits, idx_step, jnp.zeros((tq, LANES), I32))

    jb = j_s[...]

    m_s[...] = jnp.full(m_s.shape, NEG_BIG, F32)
    l_s[...] = jnp.zeros(l_s.shape, F32)
    acc_s[...] = jnp.zeros(acc_s.shape, F32)
    thr_w = _wide(thr, tq)
    jb_w = _wide(jb, tq)

    def att_tile(j, tab):
        off = pl.multiple_of(j * tq, tq)
        kt = keys_s[:, pl.ds(off, tq)]
        colg = col + off
        mask = ((kt > thr_w) | ((kt == thr_w) & (colg <= jb_w))) & (colg <= row + i * tq)
        kk = k_ref[0, pl.ds(off, tq), :]
        vv = v_ref[0, pl.ds(off, tq), :]
        for h in range(C_HEADS):
            hs = slice(h * C_HEAD_DIM, (h + 1) * C_HEAD_DIM)
            s = _dot_nt(q_ref[:, hs], kk[:, hs])
            if tab is not None:
                s = s + bias_s[tab, h]
            s = jnp.where(mask, s, NEG_BIG)
            m_prev = m_s[h]
            m_new = jnp.maximum(m_prev, jnp.max(s, axis=1, keepdims=True))
            alpha = jnp.exp(m_prev - m_new)
            p = jnp.exp(s - _wide(m_new, tq))
            l_s[h] = alpha * l_s[h] + jnp.sum(p, axis=1, keepdims=True)
            acc_s[h] = alpha * acc_s[h] + _dot(p.astype(BF16), vv[:, hs])
            m_s[h] = m_new

    def far_tile(j, carry):
        att_tile(j, None)
        return carry

    lax.fori_loop(0, jnp.maximum(i - 1, 0), far_tile, 0)

    @pl.when(i >= 1)
    def _():
        att_tile(i - 1, 0)

    att_tile(i, 1)
    for h in range(C_HEADS):
        o_ref[:, h * C_HEAD_DIM:(h + 1) * C_HEAD_DIM] = (acc_s[h] / l_s[h]).astype(o_ref.dtype)


def dsa_prompt(q16, qi16, s32, k16, v16, kit16, rel_bias, *, n_seq, tq=256):
    m = q16.shape[0]
    seq = m // n_seq
    nq = seq // tq
    nbits = int(math.log2(seq))
    assert 2 ** nbits == seq and seq % tq == 0
    kern = functools.partial(_dsa_prompt_kernel, tq=tq, nbits=nbits)
    once = pl.Buffered(1)
    return pl.pallas_call(
        kern,
        grid=(n_seq, nq),
        in_specs=[pl.BlockSpec((tq, C_WIDTH), lambda b, i: (b * nq + i, 0)),
                  pl.BlockSpec((tq, IDX_HEADS * LANES), lambda b, i: (b * nq + i, 0)),
                  pl.BlockSpec((tq, LANES), lambda b, i: (b * nq + i, 0)),
                  pl.BlockSpec((1, seq, C_WIDTH), lambda b, i: (b, 0, 0), pipeline_mode=once),
                  pl.BlockSpec((1, seq, C_WIDTH), lambda b, i: (b, 0, 0), pipeline_mode=once),
                  pl.BlockSpec((1, LANES, seq), lambda b, i: (b, 0, 0), pipeline_mode=once),
                  pl.BlockSpec(memory_space=pltpu.SMEM)],
        out_specs=pl.BlockSpec((tq, C_WIDTH), lambda b, i: (b * nq + i, 0)),
        out_shape=jax.ShapeDtypeStruct((m, C_WIDTH), BF16),
        scratch_shapes=[pltpu.VMEM((tq, seq), I32),
                        pltpu.VMEM((IDX_HEADS, tq, LANES), F32),
                        pltpu.VMEM((2, C_HEADS, tq, tq), F32),
                        pltpu.VMEM((C_HEADS, tq, LANES), F32),
                        pltpu.VMEM((C_HEADS, tq, LANES), F32),
                        pltpu.VMEM((C_HEADS, tq, C_HEAD_DIM), F32),
                        pltpu.VMEM((tq, LANES), I32)],
        compiler_params=_cparams(("arbitrary", "arbitrary")),
        name="dsa_prompt",
    )(q16, qi16, s32, k16, v16, kit16, rel_bias)


NEW_ROWS = 16


def _dsa_sample_kernel(pt_ref, *refs, n_pages, t_new):
    ki_refs = refs[0:n_pages]
    k_refs = refs[n_pages:2 * n_pages]
    v_refs = refs[2 * n_pages:3 * n_pages]
    (qit_ref, wrow_ref, qbd_ref, kin_ref, knew_ref, vnew_ref, rb_ref, o_ref,
     kiall_s, kall_s, vall_s, lg_s, keyr_s, pr_s, tab_s) = refs[3 * n_pages:]
    b = pl.program_id(0)
    past = n_pages * PAGE
    nkeys = past + PAGE
    nth = t_new * C_HEADS
    lane = lax.broadcasted_iota(I32, (PAGE, LANES), 1)
    rowp = lax.broadcasted_iota(I32, (PAGE, LANES), 0)

    @pl.when(b == 0)
    def _():
        kiall_s[past:nkeys, :] = jnp.zeros((PAGE, LANES), BF16)
        kall_s[past:nkeys, :] = jnp.zeros((PAGE, C_WIDTH), BF16)
        vall_s[past:nkeys, :] = jnp.zeros((PAGE, C_WIDTH), BF16)
        t_of = lane // C_HEADS
        tab0 = jnp.zeros((PAGE, LANES), F32)
        tab1 = jnp.zeros((PAGE, LANES), F32)
        for h in range(C_HEADS):
            sel = (lane % C_HEADS) == h
            tab0 = jnp.where(sel, _bias_table(PAGE + t_of - rowp, rb_ref, h), tab0)
            tab1 = jnp.where(sel, _bias_table(t_of - rowp, rb_ref, h), tab1)
        tab_s[0] = tab0
        tab_s[1] = tab1

    zpad = jnp.zeros((PAGE, LANES - IDX_DIM), F32)
    for p in range(n_pages):
        rs = slice(p * PAGE, (p + 1) * PAGE)
        kiall_s[rs, :] = jnp.concatenate([ki_refs[p][0], zpad], axis=1).astype(BF16)
        kall_s[rs, :] = k_refs[p][0].astype(BF16)
        vall_s[rs, :] = v_refs[p][0].astype(BF16)
    kiall_s[past:past + NEW_ROWS, :] = jnp.concatenate(
        [kin_ref[0], jnp.zeros((NEW_ROWS, LANES - IDX_DIM), F32)], axis=1).astype(BF16)
    kall_s[past:past + NEW_ROWS, :] = knew_ref[0]
    vall_s[past:past + NEW_ROWS, :] = vnew_ref[0]

    d = _dot(kiall_s[...], qit_ref[0])
    r = jnp.maximum(d, 0.0) * wrow_ref[0]
    gi = lax.broadcasted_iota(I32, (LANES, LANES), 0)
    gj = lax.broadcasted_iota(I32, (LANES, LANES), 1)
    gsum = jnp.where((gi // C_HEADS == gj // C_HEADS) & (gi < nth) & (gj < nth), 1.0, 0.0).astype(BF16)
    hi, mid, lo = _split3(r)
    sc = _dot(hi, gsum) + _dot(mid, gsum) + _dot(lo, gsum)
    rowk = lax.broadcasted_iota(I32, (nkeys, LANES), 0)
    lanek = lax.broadcasted_iota(I32, (nkeys, LANES), 1)
    valid = (rowk < past) | (rowk - past <= lanek // C_HEADS)
    sc = jnp.where(valid, sc, -jnp.inf)
    keyt = _order_key(sc)

    for p in range(nkeys // PAGE):
        keyr_s[:, p * PAGE:(p + 1) * PAGE] = _order_key(sc[p * PAGE:(p + 1) * PAGE, :].T[0:nth, :])
    kr = keyr_s[...]

    def bit_step(bi, thr):
        cand = thr + lax.shift_left(jnp.int32(1), 31 - bi)
        cnt = jnp.sum(jnp.where(kr >= cand, 1, 0), axis=1, keepdims=True)
        return jnp.where(cnt >= TOPK, cand, thr)

    thr = lax.fori_loop(0, 32, bit_step, jnp.full((nth, 1), INT_MIN, I32))
    need = TOPK - jnp.sum(jnp.where(kr > thr, 1, 0), axis=1, keepdims=True)
    colr = lax.broadcasted_iota(I32, (nth, nkeys), 1)
    nbits = int(math.ceil(math.log2(nkeys)))

    def idx_step(bi, jv):
        cand = jv + lax.shift_left(jnp.int32(1), nbits - 1 - bi)
        cnt = jnp.sum(jnp.where((kr == thr) & (colr < cand), 1, 0), axis=1, keepdims=True)
        return jnp.where(cnt <= need - 1, cand, jv)

    jv = lax.fori_loop(0, nbits, idx_step, jnp.zeros((nth, 1), I32))
    diag = lax.broadcasted_iota(I32, (nth, LANES), 0) == lax.broadcasted_iota(I32, (nth, LANES), 1)
    thr_l = jnp.sum(jnp.where(diag, thr, 0), axis=0, keepdims=True)
    jv_l = jnp.sum(jnp.where(diag, jv, 0), axis=0, keepdims=True)
    mask = ((keyt > thr_l) | ((keyt == thr_l) & (rowk <= jv_l))) & valid

    lg_s[...] = _dot(kall_s[...], qbd_ref[0])
    lg_s[past - PAGE:past, :] += tab_s[0]
    lg_s[past:nkeys, :] += tab_s[1]
    lg = jnp.where(mask, lg_s[...], NEG_BIG)
    mx = jnp.max(lg, axis=0, keepdims=True)
    lg_s[...] = jnp.exp(lg - mx)
    for p in range(nkeys // PAGE):
        pr_s[:, p * PAGE:(p + 1) * PAGE] = lg_s[p * PAGE:(p + 1) * PAGE, :].T[0:nth, :]
    pr = pr_s[...]
    den = jnp.sum(pr, axis=1, keepdims=True)
    full = _dot(pr.astype(BF16), vall_s[...]) / den
    own = (lax.broadcasted_iota(I32, (nth, C_WIDTH), 1) // C_HEAD_DIM
           == lax.broadcasted_iota(I32, (nth, C_WIDTH), 0) % C_HEADS)
    per_tok = jnp.sum(jnp.where(own, full, 0.0).reshape(t_new, C_HEADS, C_WIDTH), axis=1)
    o_ref[0] = jnp.concatenate([per_tok, jnp.zeros((8 - t_new, C_WIDTH), F32)], axis=0)


def sample_dsa_operands(q16, qi16, s32, k16, v16, t_new, db):
    nth = t_new * C_HEADS
    lane_pad = ((0, 0), (0, 0), (0, LANES - nth))
    row_pad = ((0, 0), (0, NEW_ROWS - t_new), (0, 0))
    q = q16.reshape(t_new, db, C_HEADS, C_HEAD_DIM).transpose(1, 2, 3, 0)
    eye = jnp.eye(C_HEADS, dtype=q16.dtype)
    qbd = (q[:, :, :, :, None] * eye[None, :, None, None, :]).reshape(db, C_WIDTH, nth)
    qit = qi16.reshape(t_new, db, IDX_HEADS, LANES).transpose(1, 3, 0, 2).reshape(db, LANES, nth)
    wscale = IDX_DIM ** -0.5 * IDX_HEADS ** -0.5
    wrow = (s32[:, IDX_DIM:IDX_DIM + IDX_HEADS].reshape(t_new, db, IDX_HEADS).transpose(1, 0, 2)
            .reshape(db, 1, nth) * wscale)
    by_seq = lambda a: jnp.pad(a.reshape(t_new, db, a.shape[-1]).transpose(1, 0, 2), row_pad)
    return (jnp.pad(qit, lane_pad), jnp.pad(wrow, lane_pad), jnp.pad(qbd, lane_pad),
            by_seq(s32[:, :IDX_DIM]), by_seq(k16), by_seq(v16))


def dsa_sample(page_table, pool_ki, pool_k, pool_v, qit16, wrow, qbd16, kin, knew16, vnew16, rel_bias, *, t_new):
    db, n_pages = page_table.shape
    nkeys = (n_pages + 1) * PAGE
    nth = t_new * C_HEADS
    kern = functools.partial(_dsa_sample_kernel, n_pages=n_pages, t_new=t_new)

    def page_spec(p, width):
        return pl.BlockSpec((1, PAGE, width), lambda b, pt: (pt[b, p], 0, 0))

    per_seq = lambda shape: pl.BlockSpec((1,) + shape, lambda b, pt: (b, 0, 0))
    in_specs = ([page_spec(p, IDX_DIM) for p in range(n_pages)]
                + [page_spec(p, C_WIDTH) for p in range(n_pages)]
                + [page_spec(p, C_WIDTH) for p in range(n_pages)]
                + [per_seq((LANES, LANES)), per_seq((1, LANES)), per_seq((C_WIDTH, LANES)),
                   per_seq((NEW_ROWS, IDX_DIM)), per_seq((NEW_ROWS, C_WIDTH)), per_seq((NEW_ROWS, C_WIDTH)),
                   pl.BlockSpec(memory_space=pltpu.SMEM)])
    grid_spec = pltpu.PrefetchScalarGridSpec(
        num_scalar_prefetch=1,
        grid=(db,),
        in_specs=in_specs,
        out_specs=pl.BlockSpec((1, 8, C_WIDTH), lambda b, pt: (b, 0, 0)),
        scratch_shapes=[pltpu.VMEM((nkeys, LANES), BF16),
                        pltpu.VMEM((nkeys, C_WIDTH), BF16),
                        pltpu.VMEM((nkeys, C_WIDTH), BF16),
                        pltpu.VMEM((nkeys, LANES), F32),
                        pltpu.VMEM((nth, nkeys), I32),
                        pltpu.VMEM((nth, nkeys), F32),
                        pltpu.VMEM((2, PAGE, LANES), F32)])
    return pl.pallas_call(
        kern,
        grid_spec=grid_spec,
        out_shape=jax.ShapeDtypeStruct((db, 8, C_WIDTH), F32),
        compiler_params=_cparams(("arbitrary",)),
        name="dsa_sample",
    )(page_table, *([pool_ki] * n_pages), *([pool_k] * n_pages), *([pool_v] * n_pages),
      qit16, wrow, qbd16, kin, knew16, vnew16, rel_bias)


PROMPT_TM = 512
FFN_CHUNK = 256
ODD_COLS = 3 * C_WIDTH + IDX_HEADS * LANES + LANES


def _odd_weight(w):
    d = w.shape[0]
    qkv = w[:, :3 * C_WIDTH]
    qi = w[:, 3 * C_WIDTH:3 * C_WIDTH + IDX_HEADS * IDX_DIM].reshape(d, IDX_HEADS, IDX_DIM)
    qi = jnp.pad(qi, ((0, 0), (0, 0), (0, LANES - IDX_DIM))).reshape(d, IDX_HEADS * LANES)
    tail = w[:, 3 * C_WIDTH + IDX_HEADS * IDX_DIM:]
    tail = jnp.pad(tail, ((0, 0), (0, LANES - tail.shape[1])))
    return jnp.concatenate([qkv, qi, tail], axis=1).astype(BF16)


ODD_SEGS = ((0, C_WIDTH, C_HEAD_DIM ** -0.5, (0,)),
            (C_WIDTH, C_WIDTH, 1.0, (1, 2)),
            (2 * C_WIDTH, C_WIDTH, 1.0, (3, 4)),
            (3 * C_WIDTH, IDX_HEADS * LANES, 1.0, (5,)),
            (3 * C_WIDTH + IDX_HEADS * LANES, LANES, 1.0, (6,)))
ODD_OUTS = ((C_WIDTH, BF16), (C_WIDTH, F32), (C_WIDTH, BF16), (C_WIDTH, F32), (C_WIDTH, BF16),
            (IDX_HEADS * LANES, BF16), (LANES, F32))
EVEN_COLS = 2 * A_WIDTH + 4 * B_WIDTH


def kernel(x_prompt, x_sample, state_conv_a, state_hgrn, cache_k, cache_v, cache_ki, state_ffn_conv, page_table, norm_mix, norm_ffn, norm_final, w_in_even, w_in_odd, w_out, conv_a_w, conv_a_b, ln_a_g, ln_a_b, hgrn_lb, hgrn_gn, rel_bias, w_up, ffn_conv_w, ffn_conv_b, w_down):
    bsz, seq, d = x_prompt.shape
    db, t_new, _ = x_sample.shape
    depth = norm_mix.shape[0]
    n_phys = cache_k.shape[1]
    to_tm = lambda a: jnp.transpose(a, (1, 0, 2))

    xp = x_prompt.reshape(bsz * seq, d)
    xs = to_tm(x_sample).reshape(t_new * db, d)
    lbs = jnp.cumsum(jax.nn.softmax(hgrn_lb.astype(F32), axis=0), axis=0)
    lbs = lbs - lbs[0:1]
    log_lb = jnp.log(lbs)
    log_1m_lb = jnp.log1p(-lbs)
    tiles_per_seq = seq // PROMPT_TM

    p_conv, p_hg, p_k, p_v, p_ki, p_ffn = [], [], [], [], [], []
    s_conv, s_hg, s_k, s_v, s_ki, s_ffn = [], [], [], [], [], []
    for l in range(depth):
        j = l // 2
        wo16 = w_out[l].astype(BF16)
        if l % 2 == 0:
            w16 = w_in_even[j].astype(BF16)
            segs = ((0, EVEN_COLS, 1.0, (0,)),)
            (zp,) = norm_matmul(xp, norm_mix[l], w16, segs, ((EVEN_COLS, F32),), tm=PROMPT_TM)
            (zs,) = norm_matmul(xs, norm_mix[l], w16, segs, ((EVEN_COLS, F32),), tm=PROMPT_TM)
            conv = (conv_a_w[j], conv_a_b[j], ln_a_g[j], ln_a_b[j])
            ap, cp = mixer_a(zp, *conv, jnp.zeros((bsz, A_CONV - 1, A_WIDTH), F32), n_seq=bsz, dil=1, tm=PROMPT_TM)
            bp, sp = hgrn(zp, 2, log_lb[j], log_1m_lb[j], hgrn_gn[j],
                          jnp.zeros((bsz, B_HEADS, B_DK, B_DK), F32),
                          n_seq=bsz, rt=HG_CHUNK, n_real=HG_CHUNK, nsb=HG_CHUNK // HG_SUB)
            pre_t = to_tm(state_conv_a[j]).reshape(1, (A_CONV - 1) * db, A_WIDTH)
            a_s, cs = mixer_a(zs, *conv, pre_t, n_seq=1, dil=db, tm=t_new * db)
            zb = to_tm(zs.reshape(t_new, db, EVEN_COLS))[:, :, 2 * A_WIDTH:]
            zb = jnp.pad(zb, ((0, 0), (0, 8 - t_new), (0, 0))).reshape(db * 8, 4 * B_WIDTH)
            bs8, ss = hgrn(zb, 0, log_lb[j], log_1m_lb[j], hgrn_gn[j], state_hgrn[j],
                           n_seq=db, rt=8, n_real=t_new, nsb=1)
            b_s = to_tm(bs8.reshape(db, 8, B_WIDTH)[:, :t_new]).reshape(t_new * db, B_WIDTH)
            xp = outproj(xp, [ap, bp], [wo16[:A_WIDTH], wo16[A_WIDTH:]], tm=PROMPT_TM)
            xs = outproj(xs, [a_s, b_s], [wo16[:A_WIDTH], wo16[A_WIDTH:]], tm=PROMPT_TM)
            p_conv.append(cp)
            p_hg.append(sp)
            s_conv.append(to_tm(cs.reshape(A_CONV - 1, db, A_WIDTH)))
            s_hg.append(ss)
        else:
            w16 = _odd_weight(w_in_odd[j])
            q16, k32, k16, v32, v16, qi16, s32 = norm_matmul(xp, norm_mix[l], w16, ODD_SEGS, ODD_OUTS, tm=PROMPT_TM)
            kit16 = jnp.pad(jnp.transpose(s32[:, :IDX_DIM].reshape(bsz, seq, IDX_DIM), (0, 2, 1)),
                            ((0, 0), (0, LANES - IDX_DIM), (0, 0))).astype(BF16)
            mp = dsa_prompt(q16, qi16, s32, k16.reshape(bsz, seq, C_WIDTH), v16.reshape(bsz, seq, C_WIDTH),
                            kit16, rel_bias, n_seq=bsz)
            p_k.append(k32.reshape(bsz, seq, C_HEADS, C_HEAD_DIM))
            p_v.append(v32.reshape(bsz, seq, C_HEADS, C_HEAD_DIM))
            p_ki.append(s32[:, :IDX_DIM].reshape(bsz, seq, IDX_DIM))
            q16s, k32s, k16s, v32s, v16s, qi16s, s32s = norm_matmul(xs, norm_mix[l], w16, ODD_SEGS, ODD_OUTS,
                                                                   tm=PROMPT_TM)
            ops = sample_dsa_operands(q16s, qi16s, s32s, k16s, v16s, t_new, db)
            o8 = dsa_sample(page_table, cache_ki[j], cache_k[j].reshape(n_phys, PAGE, C_WIDTH),
                            cache_v[j].reshape(n_phys, PAGE, C_WIDTH), *ops, rel_bias, t_new=t_new)
            m_s = to_tm(o8[:, :t_new]).reshape(t_new * db, C_WIDTH)
            xp = outproj(xp, [mp], [wo16], tm=PROMPT_TM)
            xs = outproj(xs, [m_s], [wo16], tm=PROMPT_TM)
            s_k.append(to_tm(k32s.reshape(t_new, db, C_WIDTH)).reshape(db, t_new, C_HEADS, C_HEAD_DIM))
            s_v.append(to_tm(v32s.reshape(t_new, db, C_WIDTH)).reshape(db, t_new, C_HEADS, C_HEAD_DIM))
            s_ki.append(to_tm(s32s[:, :IDX_DIM].reshape(t_new, db, IDX_DIM)))
        wup16, cwr, cbr, wdn16 = prep_ffn_weights(w_up[l], ffn_conv_w[l], ffn_conv_b[l], w_down[l], FFN_CHUNK)
        xp, fcp = ffn(xp, norm_ffn[l], wup16, cwr, cbr, wdn16, jnp.zeros((bsz, 2, 2 * D_FF), F32),
                      n_seq=bsz, dil=1, tm=PROMPT_TM, fc=FFN_CHUNK)
        pre_t = arrange_cols(to_tm(state_ffn_conv[l]).reshape(1, 2 * db, 2 * D_FF), FFN_CHUNK)
        xs, fcs = ffn(xs, norm_ffn[l], wup16, cwr, cbr, wdn16, pre_t, n_seq=1, dil=db, tm=t_new * db, fc=FFN_CHUNK)
        p_ffn.append(unarrange_cols(fcp[tiles_per_seq - 1::tiles_per_seq], FFN_CHUNK))
        s_ffn.append(to_tm(unarrange_cols(fcs, FFN_CHUNK).reshape(2, db, 2 * D_FF)))
    y_prompt = rmsnorm_call(xp, norm_final, tm=PROMPT_TM).reshape(bsz, seq, d)
    y_sample = to_tm(rmsnorm_call(xs, norm_final, tm=PROMPT_TM).reshape(t_new, db, d))
    return (y_prompt, y_sample,
            jnp.stack(p_conv), jnp.stack(p_hg), jnp.stack(p_k), jnp.stack(p_v), jnp.stack(p_ki), jnp.stack(p_ffn),
            jnp.stack(s_conv), jnp.stack(s_hg), jnp.stack(s_k), jnp.stack(s_v), jnp.stack(s_ki), jnp.stack(s_ffn))
```

```python
import functools
import math

import jax
import jax.numpy as jnp
from jax import lax
from jax.experimental import pallas as pl
from jax.experimental.pallas import tpu as pltpu

F32 = jnp.float32
BF16 = jnp.bfloat16
I32 = jnp.int32

EPS = 1e-6
D_MODEL = 1024
A_WIDTH = 512
A_CONV = 31
B_HEADS = 4
B_DK = 128
B_WIDTH = 512
C_HEADS = 8
C_HEAD_DIM = 128
C_WIDTH = 1024
IDX_HEADS = 8
IDX_DIM = 64
TOPK = 256
REL_BUCKETS = 32
REL_MAX_DIST = 128
D_FF = 2816
PAGE = 128

LANES = 128
VMEM_LIMIT = 60 * 1024 * 1024
NEG_BIG = -1e30
INT_MIN = -2 ** 31


def _cparams(sem):
    return pltpu.CompilerParams(dimension_semantics=sem, vmem_limit_bytes=VMEM_LIMIT)


def _rms(x, g):
    return x * lax.rsqrt(jnp.mean(x * x, axis=-1, keepdims=True) + EPS) * g


def _sigmoid(x):
    return 1.0 / (1.0 + jnp.exp(-x))


def _dot(a, b):
    return jnp.dot(a, b, preferred_element_type=F32)


def _dot_nt(a, b):
    return lax.dot_general(a, b, (((1,), (1,)), ((), ())), preferred_element_type=F32)


def _norm_matmul_kernel(x_ref, g_ref, w_ref, *out_refs, segs):
    hb = _rms(x_ref[...], g_ref[...]).astype(BF16)
    for start, width, scale, outs in segs:
        for c0 in range(0, width, 512):
            cw = min(512, width - c0)
            r = _dot(hb, w_ref[:, start + c0:start + c0 + cw])
            if scale != 1.0:
                r = r * scale
            for oi in outs:
                out_refs[oi][:, c0:c0 + cw] = r.astype(out_refs[oi].dtype)


def norm_matmul(x, g, w16, segs, out_defs, tm=512):
    m, d = x.shape
    tm = min(tm, m)
    n = w16.shape[1]
    return pl.pallas_call(
        functools.partial(_norm_matmul_kernel, segs=segs),
        grid=(m // tm,),
        in_specs=[pl.BlockSpec((tm, d), lambda i: (i, 0)),
                  pl.BlockSpec((1, d), lambda i: (0, 0)),
                  pl.BlockSpec((d, n), lambda i: (0, 0))],
        out_specs=[pl.BlockSpec((tm, wd), lambda i: (i, 0)) for wd, _ in out_defs],
        out_shape=[jax.ShapeDtypeStruct((m, wd), dt) for wd, dt in out_defs],
        compiler_params=_cparams(("arbitrary",)),
        name="norm_matmul",
    )(x, g.reshape(1, d), w16)


def _outproj_kernel(*refs, n_in):
    x_ref = refs[0]
    o_ref = refs[-1]
    acc = x_ref[...]
    for i in range(n_in):
        acc = acc + _dot(refs[1 + 2 * i][...].astype(BF16), refs[2 + 2 * i][...])
    o_ref[...] = acc


def outproj(x, ms, ws, tm=512):
    m, d = x.shape
    tm = min(tm, m)
    in_specs = [pl.BlockSpec((tm, d), lambda i: (i, 0))]
    args = [x]
    for mm, w in zip(ms, ws):
        in_specs.append(pl.BlockSpec((tm, mm.shape[1]), lambda i: (i, 0)))
        in_specs.append(pl.BlockSpec(w.shape, lambda i: (0, 0)))
        args += [mm, w]
    return pl.pallas_call(
        functools.partial(_outproj_kernel, n_in=len(ms)),
        grid=(m // tm,),
        in_specs=in_specs,
        out_specs=pl.BlockSpec((tm, d), lambda i: (i, 0)),
        out_shape=jax.ShapeDtypeStruct((m, d), F32),
        compiler_params=_cparams(("arbitrary",)),
        name="outproj",
    )(*args)


def _ffn_kernel(x_ref, g_ref, wup_ref, cw_ref, cb_ref, wdn_ref, pre_ref, o_ref, npre_ref,
                h_s, acc_s, ubuf_s, carry_s, *, tm, fc, dil, pad, tiles_per_seq):
    i = pl.program_id(0)
    c = pl.program_id(1)
    nc = pl.num_programs(1)

    @pl.when(c == 0)
    def _():
        h_s[...] = _rms(x_ref[...], g_ref[...]).astype(BF16)

    u = _dot(h_s[...], wup_ref[...])
    ubuf_s[pad:pad + tm, :] = u
    first = (i % tiles_per_seq) == 0

    @pl.when(first)
    def _():
        ubuf_s[pad - 2 * dil:pad, :] = pre_ref[0]

    @pl.when(jnp.logical_not(first))
    def _():
        ubuf_s[pad - 2 * dil:pad, :] = carry_s[c]

    tail = ubuf_s[pad + tm - 2 * dil:pad + tm, :]
    carry_s[c] = tail
    npre_ref[0] = tail
    cw = cw_ref[0]
    uc = (u * cw[2:3, :] + ubuf_s[pad - dil:pad - dil + tm, :] * cw[1:2, :]
          + ubuf_s[pad - 2 * dil:pad - 2 * dil + tm, :] * cw[0:1, :] + cb_ref[0])
    gate = uc[:, :fc]
    val = uc[:, fc:]
    act = (gate * _sigmoid(gate) * val).astype(BF16)
    part = _dot(act, wdn_ref[...])

    @pl.when(c == 0)
    def _():
        acc_s[...] = part

    @pl.when(c > 0)
    def _():
        acc_s[...] += part

    @pl.when(c == nc - 1)
    def _():
        o_ref[...] = x_ref[...] + acc_s[...]


def arrange_cols(a, fc):
    nc = D_FF // fc
    lead = a.shape[:-1]
    return a.reshape(lead + (2, nc, fc)).swapaxes(-3, -2).reshape(lead + (nc * 2 * fc,))


def unarrange_cols(a, fc):
    nc = D_FF // fc
    lead = a.shape[:-1]
    return a.reshape(lead + (nc, 2, fc)).swapaxes(-3, -2).reshape(lead + (2 * D_FF,))


def prep_ffn_weights(w_up, cw, cb, w_dn, fc):
    nc = D_FF // fc
    wup16 = arrange_cols(w_up, fc).astype(BF16)
    cwr = arrange_cols(cw, fc).reshape(cw.shape[0], nc, 2 * fc).swapaxes(0, 1)
    cbr = arrange_cols(cb, fc).reshape(nc, 1, 2 * fc)
    return wup16, cwr, cbr, w_dn.astype(BF16)


def ffn(x, g, wup16, cw, cb, wdn16, pre, *, n_seq, dil, tm, fc=256):
    m, d = x.shape
    nc = D_FF // fc
    tiles_per_seq = (m // n_seq) // tm
    pad = max(8, 2 * dil)
    kern = functools.partial(_ffn_kernel, tm=tm, fc=fc, dil=dil, pad=pad, tiles_per_seq=tiles_per_seq)
    return pl.pallas_call(
        kern,
        grid=(m // tm, nc),
        in_specs=[pl.BlockSpec((tm, d), lambda i, c: (i, 0)),
                  pl.BlockSpec((1, d), lambda i, c: (0, 0)),
                  pl.BlockSpec((d, 2 * fc), lambda i, c: (0, c)),
                  pl.BlockSpec((1, 3, 2 * fc), lambda i, c: (c, 0, 0)),
                  pl.BlockSpec((1, 1, 2 * fc), lambda i, c: (c, 0, 0)),
                  pl.BlockSpec((fc, d), lambda i, c: (c, 0)),
                  pl.BlockSpec((1, 2 * dil, 2 * fc), lambda i, c: (i // tiles_per_seq, 0, c))],
        out_specs=[pl.BlockSpec((tm, d), lambda i, c: (i, 0)),
                   pl.BlockSpec((1, 2 * dil, 2 * fc), lambda i, c: (i, 0, c))],
        out_shape=[jax.ShapeDtypeStruct((m, d), F32),
                   jax.ShapeDtypeStruct((m // tm, 2 * dil, nc * 2 * fc), F32)],
        scratch_shapes=[pltpu.VMEM((tm, d), BF16),
                        pltpu.VMEM((tm, d), F32),
                        pltpu.VMEM((pad + tm, 2 * fc), F32),
                        pltpu.VMEM((nc, 2 * dil, 2 * fc), F32)],
        compiler_params=_cparams(("arbitrary", "arbitrary")),
        name="conv_ffn",
    )(x, g.reshape(1, d), wup16, cw, cb, wdn16, pre)


def _rmsnorm_kernel(x_ref, g_ref, o_ref):
    o_ref[...] = _rms(x_ref[...], g_ref[...])


def rmsnorm_call(x, g, tm=512):
    m, d = x.shape
    tm = min(tm, m)
    return pl.pallas_call(
        _rmsnorm_kernel,
        grid=(m // tm,),
        in_specs=[pl.BlockSpec((tm, d), lambda i: (i, 0)), pl.BlockSpec((1, d), lambda i: (0, 0))],
        out_specs=pl.BlockSpec((tm, d), lambda i: (i, 0)),
        out_shape=jax.ShapeDtypeStruct((m, d), F32),
        compiler_params=_cparams(("arbitrary",)),
        name="final_rmsnorm",
    )(x, g.reshape(1, d))


def _mixer_a_kernel(val_ref, gate_ref, cw_ref, cb_ref, lg_ref, lb_ref, pre_ref, o_ref, npre_ref,
                    gbuf_s, *, tm, dil, pad, tiles_per_seq):
    i = pl.program_id(0)
    first = (i % tiles_per_seq) == 0
    hist = (A_CONV - 1) * dil

    @pl.when(first)
    def _():
        gbuf_s[pad - hist:pad, :] = pre_ref[0]

    glu = val_ref[...] * _sigmoid(gate_ref[...])
    gbuf_s[pad:pad + tm, :] = glu
    npre_ref[0] = gbuf_s[pad + tm - hist:pad + tm, :]
    acc = jnp.zeros((tm, A_WIDTH), F32) + cb_ref[...]
    for j in range(A_CONV):
        s0 = pad - hist + j * dil
        acc = acc + gbuf_s[s0:s0 + tm, :] * cw_ref[j:j + 1, :]
    mu = jnp.mean(acc, axis=-1, keepdims=True)
    xc = acc - mu
    var = jnp.mean(xc * xc, axis=-1, keepdims=True)
    y = xc * lax.rsqrt(var + EPS) * lg_ref[...] + lb_ref[...]
    o_ref[...] = (y * _sigmoid(y)).astype(o_ref.dtype)
    if tiles_per_seq > 1:
        gbuf_s[0:pad, :] = gbuf_s[tm:tm + pad, :]


def mixer_a(z, cw, cb, lg, lb, pre, *, n_seq, dil, tm):
    m = z.shape[0]
    tiles_per_seq = (m // n_seq) // tm
    hist = (A_CONV - 1) * dil
    pad = 32 if dil == 1 else hist
    kern = functools.partial(_mixer_a_kernel, tm=tm, dil=dil, pad=pad, tiles_per_seq=tiles_per_seq)
    vec = lambda a: a.reshape(1, A_WIDTH)
    return pl.pallas_call(
        kern,
        grid=(m // tm,),
        in_specs=[pl.BlockSpec((tm, A_WIDTH), lambda i: (i, 0)),
                  pl.BlockSpec((tm, A_WIDTH), lambda i: (i, 1)),
                  pl.BlockSpec((A_CONV, A_WIDTH), lambda i: (0, 0)),
                  pl.BlockSpec((1, A_WIDTH), lambda i: (0, 0)),
                  pl.BlockSpec((1, A_WIDTH), lambda i: (0, 0)),
                  pl.BlockSpec((1, A_WIDTH), lambda i: (0, 0)),
                  pl.BlockSpec((1, hist, A_WIDTH), lambda i: (i // tiles_per_seq, 0, 0))],
        out_specs=[pl.BlockSpec((tm, A_WIDTH), lambda i: (i, 0)),
                   pl.BlockSpec((1, hist, A_WIDTH), lambda i: (i // tiles_per_seq, 0, 0))],
        out_shape=[jax.ShapeDtypeStruct((m, A_WIDTH), BF16),
                   jax.ShapeDtypeStruct((n_seq, hist, A_WIDTH), F32)],
        scratch_shapes=[pltpu.VMEM((pad + tm, A_WIDTH), F32)],
        compiler_params=_cparams(("arbitrary",)),
        name="mixer_a",
    )(z, z, cw, vec(cb), vec(lg), vec(lb), pre)


HG_CHUNK = 128
HG_SUB = 16


def _split3(x):
    hi = x.astype(BF16)
    r1 = x - hi.astype(F32)
    mid = r1.astype(BF16)
    lo = (r1 - mid.astype(F32)).astype(BF16)
    return hi, mid, lo


def _hgrn_chunk(qr, fr, ir, gr, llb, l1m, gn, st_s, b_s, q_s, k_s, v_s, oacc_s, *, n_real, nsb):
    c = HG_CHUNK
    q = qr * _sigmoid(qr)
    ls = jnp.minimum(fr, 0.0) - jnp.log1p(jnp.exp(-jnp.abs(fr)))
    bt = l1m + ls
    logf = jnp.maximum(llb, bt) + jnp.log1p(jnp.exp(-jnp.abs(llb - bt)))
    kk = 1.0 - jnp.exp(logf)
    if n_real < c:
        real = lax.broadcasted_iota(I32, (c, B_WIDTH), 0) < n_real
        logf = jnp.where(real, logf, 0.0)
        kk = jnp.where(real, kk, 0.0)
    ri = lax.broadcasted_iota(I32, (c, c), 0)
    ci = lax.broadcasted_iota(I32, (c, c), 1)
    tril = jnp.where(ci <= ri, 1.0, 0.0).astype(BF16)
    hi, mid, lo = _split3(logf)
    b = _dot(tril, hi) + _dot(tril, mid) + _dot(tril, lo)
    b_s[...] = b
    q_s[...] = q
    k_s[...] = kk
    v_s[...] = ir
    sub = HG_SUB
    row16 = lax.broadcasted_iota(I32, (sub, 1), 0)
    outs = []
    for h in range(B_HEADS):
        sl = slice(h * B_DK, (h + 1) * B_DK)
        st = st_s[h]
        oacc_s[:, sl] = _dot_nt((q[:, sl] * jnp.exp(b[:, sl])).astype(BF16), st.astype(BF16))
        for i in range(nsb):
            r0 = i * sub
            qb = q_s[r0:r0 + sub, sl]
            bb = b_s[r0:r0 + sub, sl]
            o_blk = jnp.zeros((sub, B_DK), F32)
            if i > 0:
                ref_b = b_s[r0 - 1:r0, sl]
                qt = (qb * jnp.exp(bb - ref_b)).astype(BF16)
                kt = (k_s[0:r0, sl] * jnp.exp(ref_b - b_s[0:r0, sl])).astype(BF16)
                att = _dot_nt(qt, kt)
                o_blk = o_blk + _dot(att.astype(BF16), v_s[0:r0, sl].astype(BF16))
            for s in range(sub):
                r = r0 + s
                if r >= n_real:
                    continue
                p = qb * jnp.exp(bb - b_s[r:r + 1, sl]) * k_s[r:r + 1, sl]
                a = jnp.sum(p, axis=1, keepdims=True)
                a = jnp.where(row16 >= s, a, 0.0)
                o_blk = o_blk + a * v_s[r:r + 1, sl]
            oacc_s[r0:r0 + sub, sl] += o_blk
        o = oacc_s[:, sl]
        on = o * lax.rsqrt(jnp.mean(o * o, axis=-1, keepdims=True) + EPS) * gn
        g = gr[:, sl]
        outs.append(on * (g * _sigmoid(g)))
        bl = b_s[c - 1:c, sl]
        kd = (k_s[:, sl] * jnp.exp(bl - b_s[:, sl])).astype(BF16)
        vt = v_s[:, sl].T.astype(BF16)
        st_s[h] = st * jnp.exp(bl) + _dot(vt, kd)
    return jnp.concatenate(outs, axis=1)


def _hgrn_kernel(q_ref, f_ref, i_ref, g_ref, llb_ref, l1m_ref, gn_ref, s0_ref, o_ref, sout_ref,
                 st_s, b_s, q_s, k_s, v_s, oacc_s, *, rt, n_real, nsb, tiles_per_seq):
    c = HG_CHUNK
    first = (pl.program_id(0) % tiles_per_seq) == 0

    @pl.when(first)
    def _():
        for h in range(B_HEADS):
            st_s[h] = s0_ref[0, h].T

    rows = min(rt, c)
    for ch in range(max(1, rt // c)):
        rs = slice(ch * c, ch * c + rows)

        def load(ref):
            x = ref[rs, :]
            if rows < c:
                x = jnp.concatenate([x, jnp.zeros((c - rows, B_WIDTH), F32)], axis=0)
            return x

        out = _hgrn_chunk(load(q_ref), load(f_ref), load(i_ref), load(g_ref),
                          llb_ref[...], l1m_ref[...], gn_ref[...],
                          st_s, b_s, q_s, k_s, v_s, oacc_s, n_real=n_real, nsb=nsb)
        o_ref[rs, :] = out[0:rows, :].astype(o_ref.dtype)
    for h in range(B_HEADS):
        sout_ref[0, h] = st_s[h].T


def hgrn(z, col0, llb, l1m, gn, s0, *, n_seq, rt, n_real, nsb):
    m = z.shape[0]
    tiles_per_seq = (m // n_seq) // rt
    kern = functools.partial(_hgrn_kernel, rt=rt, n_real=n_real, nsb=nsb, tiles_per_seq=tiles_per_seq)
    c = HG_CHUNK
    zspec = lambda k: pl.BlockSpec((rt, B_WIDTH), lambda i: (i, col0 + k))
    state_spec = pl.BlockSpec((1, B_HEADS, B_DK, B_DK), lambda i: (i // tiles_per_seq, 0, 0, 0))
    return pl.pallas_call(
        kern,
        grid=(m // rt,),
        in_specs=[zspec(0), zspec(1), zspec(2), zspec(3),
                  pl.BlockSpec((1, B_WIDTH), lambda i: (0, 0)),
                  pl.BlockSpec((1, B_WIDTH), lambda i: (0, 0)),
                  pl.BlockSpec((1, B_DK), lambda i: (0, 0)),
                  state_spec],
        out_specs=[pl.BlockSpec((rt, B_WIDTH), lambda i: (i, 0)), state_spec],
        out_shape=[jax.ShapeDtypeStruct((m, B_WIDTH), BF16),
                   jax.ShapeDtypeStruct((n_seq, B_HEADS, B_DK, B_DK), F32)],
        scratch_shapes=[pltpu.VMEM((B_HEADS, B_DK, B_DK), F32),
                        pltpu.VMEM((c, B_WIDTH), F32),
                        pltpu.VMEM((c, B_WIDTH), F32),
                        pltpu.VMEM((c, B_WIDTH), F32),
                        pltpu.VMEM((c, B_WIDTH), F32),
                        pltpu.VMEM((c, B_WIDTH), F32)],
        compiler_params=_cparams(("arbitrary",)),
        name="hgrn2",
    )(z, z, z, z, llb.reshape(1, B_WIDTH), l1m.reshape(1, B_WIDTH), gn.reshape(1, B_DK), s0)


def _order_key(x):
    b = lax.bitcast_convert_type(x + 0.0, I32)
    return jnp.where(b < 0, b ^ 0x7FFFFFFF, b)


def _t5_bucket(dist):
    n = jnp.maximum(dist, 0)
    max_exact = REL_BUCKETS // 2
    nf = jnp.maximum(n, 1).astype(F32)
    large = max_exact + (jnp.log(nf / max_exact) / math.log(REL_MAX_DIST / max_exact)
                         * (REL_BUCKETS - max_exact)).astype(I32)
    large = jnp.minimum(large, REL_BUCKETS - 1)
    return jnp.where(n < max_exact, n, large)


def _bias_table(dist, rb_ref, h):
    bucket = _t5_bucket(dist)
    last = rb_ref[REL_BUCKETS - 1, h]
    out = jnp.zeros(dist.shape, F32)
    for bk in range(REL_BUCKETS - 1):
        out = jnp.where(bucket == bk, rb_ref[bk, h] - last, out)
    return out


LOG2E = math.log2(math.e)
ATT_ROWS = 128


def _wide(x, n):
    return jnp.concatenate([x] * (n // LANES), axis=1)


def _dsa_prompt_kernel(q_ref, qi_ref, s_ref, k_ref, v_ref, kit_ref, rb_ref, o_ref,
                       keys_s, wb_s, bias_s, m_s, l_s, acc_s, j_s, mb_s, *, tq, nbits):
    b = pl.program_id(0)
    i = pl.program_id(1)
    row = lax.broadcasted_iota(I32, (tq, tq), 0)
    col = lax.broadcasted_iota(I32, (tq, tq), 1)

    @pl.when((b == 0) & (i == 0))
    def _():
        for h in range(C_HEADS):
            bias_s[0, h] = _bias_table(row + tq - col, rb_ref, h) * LOG2E
            bias_s[1, h] = _bias_table(row - col, rb_ref, h) * LOG2E

    wscale = IDX_DIM ** -0.5 * IDX_HEADS ** -0.5
    for h in range(IDX_HEADS):
        wcol = s_ref[:, IDX_DIM + h:IDX_DIM + h + 1] * wscale
        wb_s[h] = jnp.broadcast_to(wcol, (tq, LANES))

    def score_tile(j, carry):
        off = pl.multiple_of(j * tq, tq)
        kt = kit_ref[0, :, pl.ds(off, tq)]
        sc = jnp.zeros((tq, tq), F32)
        for h in range(IDX_HEADS):
            d = _dot(qi_ref[:, h * LANES:(h + 1) * LANES], kt)
            sc = sc + jnp.maximum(d, 0.0) * _wide(wb_s[h], tq)
        sc = jnp.where((j == i) & (col > row), -jnp.inf, sc)
        keys_s[:, pl.ds(off, tq)] = _order_key(sc)
        return carry

    lax.fori_loop(0, i + 1, score_tile, 0)

    def count(pred_fn):
        def body(j, acc):
            off = pl.multiple_of(j * tq, tq)
            kt = keys_s[:, pl.ds(off, tq)]
            for hf in range(tq // LANES):
                acc = acc + jnp.where(pred_fn(kt[:, hf * LANES:(hf + 1) * LANES], off + hf * LANES), 1, 0)
            return acc
        acc = lax.fori_loop(0, i + 1, body, jnp.zeros((tq, LANES), I32))
        return jnp.sum(acc, axis=1, keepdims=True)

    def bit_step(bi, carry):
        thr, n_ge = carry
        cand = thr + lax.shift_left(jnp.int32(1), 31 - bi)
        cnt = count(lambda kt, off: kt >= cand)
        ok = cnt >= TOPK
        return jnp.where(ok, cand, thr), jnp.where(ok, cnt, n_ge)

    thr, n_ge = lax.fori_loop(0, 32, bit_step, (jnp.full((tq, LANES), INT_MIN, I32),
                                                jnp.zeros((tq, 1), I32) + (i + 1) * tq))

    n_gt = count(lambda kt, off: kt > thr)
    need = TOPK - n_gt
    lane = lax.broadcasted_iota(I32, (tq, LANES), 1)
    j_s[...] = jnp.full((tq, LANES), 2 ** 30, I32)
    any_tie = jnp.max(jnp.where(n_ge > TOPK, 1.0, 0.0)) > 0.0

    @pl.when(any_tie)
    def _():
        def idx_step(bi, jv):
            cand = jv + lax.shift_left(jnp.int32(1), nbits - 1 - bi)
            cnt = count(lambda kt, off: (kt == thr) & (lane + off < cand))
            return jnp.where(cnt <= need - 1, cand, jv)
        j_s[...] = lax.fori_loop(0, nbits, idx_step, jnp.zeros((tq, LANES), I32))

    jb = j_s[...]

    m_s[...] = jnp.full(m_s.shape, NEG_BIG, F32)
    l_s[...] = jnp.zeros(l_s.shape, F32)
    acc_s[...] = jnp.zeros(acc_s.shape, F32)
    thr_w = _wide(thr, tq)
    jb_w = _wide(jb, tq)

    def att_tile(j, tab):
        off = pl.multiple_of(j * tq, tq)
        kt = keys_s[:, pl.ds(off, tq)]
        colg = col + off
        mask = ((kt > thr_w) | ((kt == thr_w) & (colg <= jb_w))) & (colg <= row + i * tq)
        mb_s[...] = jnp.where(mask, 0.0, NEG_BIG)
        for h in range(C_HEADS):
            hs = slice(h * C_HEAD_DIM, (h + 1) * C_HEAD_DIM)
            kh = k_ref[0, pl.ds(off, tq), hs]
            vh = v_ref[0, pl.ds(off, tq), hs]
            for rc in range(tq // ATT_ROWS):
                rs = slice(rc * ATT_ROWS, (rc + 1) * ATT_ROWS)
                s = _dot_nt(q_ref[rs, hs], kh) + mb_s[rs, :]
                if tab is not None:
                    s = s + bias_s[tab, h, rs, :]
                m_prev = m_s[h, rs, :]
                m_new = jnp.maximum(m_prev, jnp.max(s, axis=1, keepdims=True))
                alpha = jnp.exp2(m_prev - m_new)
                p = jnp.exp2(s - _wide(m_new, tq))
                l_s[h, rs, :] = alpha * l_s[h, rs, :] + jnp.sum(p, axis=1, keepdims=True)
                acc_s[h, rs, :] = alpha * acc_s[h, rs, :] + _dot(p.astype(BF16), vh)
                m_s[h, rs, :] = m_new

    def far_tile(j, carry):
        att_tile(j, None)
        return carry

    lax.fori_loop(0, jnp.maximum(i - 1, 0), far_tile, 0)

    @pl.when(i >= 1)
    def _():
        att_tile(i - 1, 0)

    att_tile(i, 1)
    for h in range(C_HEADS):
        o_ref[:, h * C_HEAD_DIM:(h + 1) * C_HEAD_DIM] = (acc_s[h] / l_s[h]).astype(o_ref.dtype)


def dsa_prompt(q16, qi16, s32, k16, v16, kit16, rel_bias, *, n_seq, tq=256):
    m = q16.shape[0]
    seq = m // n_seq
    nq = seq // tq
    nbits = int(math.log2(seq))
    assert 2 ** nbits == seq and seq % tq == 0
    kern = functools.partial(_dsa_prompt_kernel, tq=tq, nbits=nbits)
    once = pl.Buffered(1)
    return pl.pallas_call(
        kern,
        grid=(n_seq, nq),
        in_specs=[pl.BlockSpec((tq, C_WIDTH), lambda b, i: (b * nq + i, 0)),
                  pl.BlockSpec((tq, IDX_HEADS * LANES), lambda b, i: (b * nq + i, 0)),
                  pl.BlockSpec((tq, LANES), lambda b, i: (b * nq + i, 0)),
                  pl.BlockSpec((1, seq, C_WIDTH), lambda b, i: (b, 0, 0), pipeline_mode=once),
                  pl.BlockSpec((1, seq, C_WIDTH), lambda b, i: (b, 0, 0), pipeline_mode=once),
                  pl.BlockSpec((1, LANES, seq), lambda b, i: (b, 0, 0), pipeline_mode=once),
                  pl.BlockSpec(memory_space=pltpu.SMEM)],
        out_specs=pl.BlockSpec((tq, C_WIDTH), lambda b, i: (b * nq + i, 0)),
        out_shape=jax.ShapeDtypeStruct((m, C_WIDTH), BF16),
        scratch_shapes=[pltpu.VMEM((tq, seq), I32),
                        pltpu.VMEM((IDX_HEADS, tq, LANES), F32),
                        pltpu.VMEM((2, C_HEADS, tq, tq), F32),
                        pltpu.VMEM((C_HEADS, tq, LANES), F32),
                        pltpu.VMEM((C_HEADS, tq, LANES), F32),
                        pltpu.VMEM((C_HEADS, tq, C_HEAD_DIM), F32),
                        pltpu.VMEM((tq, LANES), I32),
                        pltpu.VMEM((tq, tq), F32)],
        compiler_params=_cparams(("arbitrary", "arbitrary")),
        name="dsa_prompt",
    )(q16, qi16, s32, k16, v16, kit16, rel_bias)


NEW_ROWS = 16


def _dsa_sample_kernel(pt_ref, *refs, n_pages, t_new):
    ki_refs = refs[0:n_pages]
    k_refs = refs[n_pages:2 * n_pages]
    v_refs = refs[2 * n_pages:3 * n_pages]
    (qi_ref, wrep_ref, q_ref, kin_ref, knew_ref, vnew_ref, rb_ref, o_ref,
     kiall_s, xk_s, xv_s, lg_s, tab_s, exp_s) = refs[3 * n_pages:]
    b = pl.program_id(0)
    past = n_pages * PAGE
    nkeys = past + PAGE
    n_tiles = n_pages + 1
    nth = t_new * C_HEADS
    pw = PAGE * C_HEADS
    rowx = lax.broadcasted_iota(I32, (nth, pw), 0)
    lanex = lax.broadcasted_iota(I32, (nth, pw), 1)
    own_head = (lanex % C_HEADS) == (rowx % C_HEADS)

    @pl.when(b == 0)
    def _():
        kiall_s[past:nkeys, :] = jnp.zeros((PAGE, LANES), BF16)
        xk_s[past * C_HEADS:nkeys * C_HEADS, :] = jnp.zeros((pw, C_HEAD_DIM), BF16)
        xv_s[past * C_HEADS:nkeys * C_HEADS, :] = jnp.zeros((pw, C_HEAD_DIM), BF16)
        ei = lax.broadcasted_iota(I32, (PAGE, pw), 0)
        ej = lax.broadcasted_iota(I32, (PAGE, pw), 1)
        exp_s[...] = jnp.where(ej // C_HEADS == ei, 1.0, 0.0).astype(BF16)
        t_of = rowx // C_HEADS
        k_of = lanex // C_HEADS
        tab0 = jnp.zeros((nth, pw), F32)
        tab1 = jnp.zeros((nth, pw), F32)
        for h in range(C_HEADS):
            sel = (lanex % C_HEADS) == h
            tab0 = jnp.where(sel, _bias_table(PAGE + t_of - k_of, rb_ref, h), tab0)
            tab1 = jnp.where(sel, _bias_table(t_of - k_of, rb_ref, h), tab1)
        tab_s[0] = tab0 * LOG2E
        tab_s[1] = tab1 * LOG2E

    zpad = jnp.zeros((PAGE, LANES - IDX_DIM), F32)
    for p in range(n_pages):
        kiall_s[p * PAGE:(p + 1) * PAGE, :] = jnp.concatenate([ki_refs[p][0, 0], zpad], axis=1).astype(BF16)
        xk_s[p * pw:(p + 1) * pw, :] = k_refs[p][0, 0].reshape(pw, C_HEAD_DIM).astype(BF16)
        xv_s[p * pw:(p + 1) * pw, :] = v_refs[p][0, 0].reshape(pw, C_HEAD_DIM).astype(BF16)
    kiall_s[past:past + NEW_ROWS, :] = jnp.concatenate(
        [kin_ref[0], jnp.zeros((NEW_ROWS, LANES - IDX_DIM), F32)], axis=1).astype(BF16)
    xk_s[past * C_HEADS:(past + NEW_ROWS) * C_HEADS, :] = knew_ref[0]
    xv_s[past * C_HEADS:(past + NEW_ROWS) * C_HEADS, :] = vnew_ref[0]

    d = _dot_nt(qi_ref[0], kiall_s[...])
    r = jnp.maximum(d, 0.0) * _wide(wrep_ref[0], nkeys)
    sc = jnp.sum(r.reshape(t_new, C_HEADS, nkeys), axis=1)
    rowr = lax.broadcasted_iota(I32, (t_new, nkeys), 0)
    colr = lax.broadcasted_iota(I32, (t_new, nkeys), 1)
    valid = (colr < past) | (colr - past <= rowr)
    kr = _order_key(jnp.where(valid, sc, -jnp.inf))

    def pick(ok1, c1, ok2, c2, ok3, c3, cur):
        return jnp.where(ok3, c3, jnp.where(ok2, c2, jnp.where(ok1, c1, cur)))

    def bit_step(bi, thr):
        one = lax.shift_left(jnp.int32(1), 30 - 2 * bi)
        c1, c2, c3 = thr + one, thr + 2 * one, thr + 3 * one
        cnt = lambda c: jnp.sum(jnp.where(kr >= c, 1, 0), axis=1, keepdims=True)
        return pick(cnt(c1) >= TOPK, c1, cnt(c2) >= TOPK, c2, cnt(c3) >= TOPK, c3, thr)

    thr = lax.fori_loop(0, 16, bit_step, jnp.full((t_new, 1), INT_MIN, I32))
    need = TOPK - jnp.sum(jnp.where(kr > thr, 1, 0), axis=1, keepdims=True)
    tie = kr == thr
    nsteps = (int(math.ceil(math.log2(nkeys))) + 1) // 2

    def idx_step(bi, jv):
        one = lax.shift_left(jnp.int32(1), 2 * (nsteps - 1 - bi))
        c1, c2, c3 = jv + one, jv + 2 * one, jv + 3 * one
        cnt = lambda c: jnp.sum(jnp.where(tie & (colr < c), 1, 0), axis=1, keepdims=True)
        return pick(cnt(c1) < need, c1, cnt(c2) < need, c2, cnt(c3) < need, c3, jv)

    jv = lax.fori_loop(0, nsteps, idx_step, jnp.zeros((t_new, 1), I32))
    sel = ((kr > thr) | (tie & (colr <= jv))) & valid
    sel = jnp.broadcast_to(jnp.where(sel, 1.0, 0.0)[:, None, :], (t_new, C_HEADS, nkeys))
    sel16 = sel.reshape(nth, nkeys).astype(BF16)

    qs = q_ref[0]
    sel_rows = jnp.concatenate([sel16[:, p * PAGE:(p + 1) * PAGE] for p in range(n_tiles)], axis=0)
    selx = _dot(sel_rows, exp_s[...])
    for p in range(n_tiles):
        s = _dot_nt(qs, xk_s[p * pw:(p + 1) * pw, :])
        if p >= n_pages - 1:
            s = s + tab_s[p - (n_pages - 1)]
        lg_s[p] = jnp.where((selx[p * nth:(p + 1) * nth, :] > 0.5) & own_head, s, NEG_BIG)
    mx = jnp.full((nth, 1), NEG_BIG, F32)
    for p in range(n_tiles):
        mx = jnp.maximum(mx, jnp.max(lg_s[p], axis=1, keepdims=True))
    den = jnp.zeros((nth, 1), F32)
    out = jnp.zeros((nth, C_HEAD_DIM), F32)
    for p in range(n_tiles):
        pr = jnp.exp2(lg_s[p] - mx)
        den = den + jnp.sum(pr, axis=1, keepdims=True)
        out = out + _dot(pr.astype(BF16), xv_s[p * pw:(p + 1) * pw, :])
    o_ref[0] = out / den


def sample_dsa_operands(q16, qi16, s32, k16, v16, t_new, db):
    nth = t_new * C_HEADS
    by_seq = lambda a: a.reshape((t_new, db) + a.shape[1:]).swapaxes(0, 1)
    qi = by_seq(qi16.reshape(t_new * db, IDX_HEADS, LANES)).reshape(db, nth, LANES)
    q = by_seq(q16.reshape(t_new * db, C_HEADS, C_HEAD_DIM)).reshape(db, nth, C_HEAD_DIM)
    wscale = IDX_DIM ** -0.5 * IDX_HEADS ** -0.5
    w = by_seq(s32[:, IDX_DIM:IDX_DIM + IDX_HEADS]).reshape(db, nth, 1) * wscale
    wrep = jnp.broadcast_to(w, (db, nth, LANES))
    row_pad = ((0, 0), (0, NEW_ROWS - t_new), (0, 0))
    kin = jnp.pad(by_seq(s32[:, :IDX_DIM]), row_pad)
    new_rows = lambda a: jnp.pad(by_seq(a), row_pad).reshape(db, NEW_ROWS * C_HEADS, C_HEAD_DIM)
    return qi, wrep, q, kin, new_rows(k16), new_rows(v16)


def dsa_sample(page_table, layer, pool_ki, pool_k, pool_v, qi16, wrep, q16, kin, knew16, vnew16, rel_bias, *, t_new):
    db, n_pages = page_table.shape
    n_tiles = n_pages + 1
    nkeys = n_tiles * PAGE
    nth = t_new * C_HEADS
    pw = PAGE * C_HEADS
    kern = functools.partial(_dsa_sample_kernel, n_pages=n_pages, t_new=t_new)

    def ki_spec(p):
        return pl.BlockSpec((1, 1, PAGE, IDX_DIM), lambda b, pt: (layer, pt[b, p], 0, 0))

    def kv_spec(p):
        return pl.BlockSpec((1, 1, PAGE, C_HEADS, C_HEAD_DIM), lambda b, pt: (layer, pt[b, p], 0, 0, 0))

    per_seq = lambda shape: pl.BlockSpec((1,) + shape, lambda b, pt: (b, 0, 0))
    in_specs = ([ki_spec(p) for p in range(n_pages)]
                + [kv_spec(p) for p in range(n_pages)]
                + [kv_spec(p) for p in range(n_pages)]
                + [per_seq((nth, LANES)), per_seq((nth, LANES)), per_seq((nth, C_HEAD_DIM)),
                   per_seq((NEW_ROWS, IDX_DIM)), per_seq((NEW_ROWS * C_HEADS, C_HEAD_DIM)),
                   per_seq((NEW_ROWS * C_HEADS, C_HEAD_DIM)),
                   pl.BlockSpec(memory_space=pltpu.SMEM)])
    grid_spec = pltpu.PrefetchScalarGridSpec(
        num_scalar_prefetch=1,
        grid=(db,),
        in_specs=in_specs,
        out_specs=pl.BlockSpec((1, nth, C_HEAD_DIM), lambda b, pt: (b, 0, 0)),
        scratch_shapes=[pltpu.VMEM((nkeys, LANES), BF16),
                        pltpu.VMEM((nkeys * C_HEADS, C_HEAD_DIM), BF16),
                        pltpu.VMEM((nkeys * C_HEADS, C_HEAD_DIM), BF16),
                        pltpu.VMEM((n_tiles, nth, pw), F32),
                        pltpu.VMEM((2, nth, pw), F32),
                        pltpu.VMEM((PAGE, pw), BF16)])
    return pl.pallas_call(
        kern,
        grid_spec=grid_spec,
        out_shape=jax.ShapeDtypeStruct((db, nth, C_HEAD_DIM), F32),
        compiler_params=_cparams(("arbitrary",)),
        name="dsa_sample",
    )(page_table, *([pool_ki] * n_pages), *([pool_k] * n_pages), *([pool_v] * n_pages),
      qi16, wrep, q16, kin, knew16, vnew16, rel_bias)


PROMPT_TM = 512
FFN_TM = 1024
FFN_CHUNK = 256
ODD_COLS = 3 * C_WIDTH + IDX_HEADS * LANES + LANES


def _odd_weight(w):
    d = w.shape[0]
    qkv = w[:, :3 * C_WIDTH]
    qi = w[:, 3 * C_WIDTH:3 * C_WIDTH + IDX_HEADS * IDX_DIM].reshape(d, IDX_HEADS, IDX_DIM)
    qi = jnp.pad(qi, ((0, 0), (0, 0), (0, LANES - IDX_DIM))).reshape(d, IDX_HEADS * LANES)
    tail = w[:, 3 * C_WIDTH + IDX_HEADS * IDX_DIM:]
    tail = jnp.pad(tail, ((0, 0), (0, LANES - tail.shape[1])))
    return jnp.concatenate([qkv, qi, tail], axis=1).astype(BF16)


ODD_SEGS = ((0, C_WIDTH, C_HEAD_DIM ** -0.5 * LOG2E, (0,)),
            (C_WIDTH, C_WIDTH, 1.0, (1, 2)),
            (2 * C_WIDTH, C_WIDTH, 1.0, (3, 4)),
            (3 * C_WIDTH, IDX_HEADS * LANES, 1.0, (5,)),
            (3 * C_WIDTH + IDX_HEADS * LANES, LANES, 1.0, (6,)))
ODD_OUTS = ((C_WIDTH, BF16), (C_WIDTH, F32), (C_WIDTH, BF16), (C_WIDTH, F32), (C_WIDTH, BF16),
            (IDX_HEADS * LANES, BF16), (LANES, F32))
EVEN_COLS = 2 * A_WIDTH + 4 * B_WIDTH


def kernel(x_prompt, x_sample, state_conv_a, state_hgrn, cache_k, cache_v, cache_ki, state_ffn_conv, page_table, norm_mix, norm_ffn, norm_final, w_in_even, w_in_odd, w_out, conv_a_w, conv_a_b, ln_a_g, ln_a_b, hgrn_lb, hgrn_gn, rel_bias, w_up, ffn_conv_w, ffn_conv_b, w_down):
    bsz, seq, d = x_prompt.shape
    db, t_new, _ = x_sample.shape
    depth = norm_mix.shape[0]
    n_phys = cache_k.shape[1]
    to_tm = lambda a: jnp.transpose(a, (1, 0, 2))

    xp = x_prompt.reshape(bsz * seq, d)
    xs = to_tm(x_sample).reshape(t_new * db, d)
    lbs = jnp.cumsum(jax.nn.softmax(hgrn_lb.astype(F32), axis=0), axis=0)
    lbs = lbs - lbs[0:1]
    log_lb = jnp.log(lbs)
    log_1m_lb = jnp.log1p(-lbs)
    tiles_per_seq = seq // FFN_TM

    p_conv, p_hg, p_k, p_v, p_ki, p_ffn = [], [], [], [], [], []
    s_conv, s_hg, s_k, s_v, s_ki, s_ffn = [], [], [], [], [], []
    for l in range(depth):
        j = l // 2
        wo16 = w_out[l].astype(BF16)
        if l % 2 == 0:
            w16 = w_in_even[j].astype(BF16)
            segs = ((0, EVEN_COLS, 1.0, (0,)),)
            (zp,) = norm_matmul(xp, norm_mix[l], w16, segs, ((EVEN_COLS, F32),), tm=PROMPT_TM)
            (zs,) = norm_matmul(xs, norm_mix[l], w16, segs, ((EVEN_COLS, F32),), tm=PROMPT_TM)
            conv = (conv_a_w[j], conv_a_b[j], ln_a_g[j], ln_a_b[j])
            ap, cp = mixer_a(zp, *conv, jnp.zeros((bsz, A_CONV - 1, A_WIDTH), F32), n_seq=bsz, dil=1, tm=PROMPT_TM)
            bp, sp = hgrn(zp, 2, log_lb[j], log_1m_lb[j], hgrn_gn[j],
                          jnp.zeros((bsz, B_HEADS, B_DK, B_DK), F32),
                          n_seq=bsz, rt=HG_CHUNK, n_real=HG_CHUNK, nsb=HG_CHUNK // HG_SUB)
            pre_t = to_tm(state_conv_a[j]).reshape(1, (A_CONV - 1) * db, A_WIDTH)
            a_s, cs = mixer_a(zs, *conv, pre_t, n_seq=1, dil=db, tm=t_new * db)
            zb = to_tm(zs.reshape(t_new, db, EVEN_COLS))[:, :, 2 * A_WIDTH:]
            zb = jnp.pad(zb, ((0, 0), (0, 8 - t_new), (0, 0))).reshape(db * 8, 4 * B_WIDTH)
            bs8, ss = hgrn(zb, 0, log_lb[j], log_1m_lb[j], hgrn_gn[j], state_hgrn[j],
                           n_seq=db, rt=8, n_real=t_new, nsb=1)
            b_s = to_tm(bs8.reshape(db, 8, B_WIDTH)[:, :t_new]).reshape(t_new * db, B_WIDTH)
            xp = outproj(xp, [ap, bp], [wo16[:A_WIDTH], wo16[A_WIDTH:]], tm=PROMPT_TM)
            xs = outproj(xs, [a_s, b_s], [wo16[:A_WIDTH], wo16[A_WIDTH:]], tm=PROMPT_TM)
            p_conv.append(cp)
            p_hg.append(sp)
            s_conv.append(to_tm(cs.reshape(A_CONV - 1, db, A_WIDTH)))
            s_hg.append(ss)
        else:
            w16 = _odd_weight(w_in_odd[j])
            q16, k32, k16, v32, v16, qi16, s32 = norm_matmul(xp, norm_mix[l], w16, ODD_SEGS, ODD_OUTS, tm=PROMPT_TM)
            kit16 = jnp.pad(jnp.transpose(s32[:, :IDX_DIM].reshape(bsz, seq, IDX_DIM), (0, 2, 1)),
                            ((0, 0), (0, LANES - IDX_DIM), (0, 0))).astype(BF16)
            mp = dsa_prompt(q16, qi16, s32, k16.reshape(bsz, seq, C_WIDTH), v16.reshape(bsz, seq, C_WIDTH),
                            kit16, rel_bias, n_seq=bsz)
            p_k.append(k32.reshape(bsz, seq, C_HEADS, C_HEAD_DIM))
            p_v.append(v32.reshape(bsz, seq, C_HEADS, C_HEAD_DIM))
            p_ki.append(s32[:, :IDX_DIM].reshape(bsz, seq, IDX_DIM))
            q16s, k32s, k16s, v32s, v16s, qi16s, s32s = norm_matmul(xs, norm_mix[l], w16, ODD_SEGS, ODD_OUTS,
                                                                   tm=PROMPT_TM)
            ops = sample_dsa_operands(q16s, qi16s, s32s, k16s, v16s, t_new, db)
            o_s = dsa_sample(page_table, j, cache_ki, cache_k, cache_v, *ops, rel_bias, t_new=t_new)
            m_s = to_tm(o_s.reshape(db, t_new, C_WIDTH)).reshape(t_new * db, C_WIDTH)
            xp = outproj(xp, [mp], [wo16], tm=PROMPT_TM)
            xs = outproj(xs, [m_s], [wo16], tm=PROMPT_TM)
            s_k.append(to_tm(k32s.reshape(t_new, db, C_WIDTH)).reshape(db, t_new, C_HEADS, C_HEAD_DIM))
            s_v.append(to_tm(v32s.reshape(t_new, db, C_WIDTH)).reshape(db, t_new, C_HEADS, C_HEAD_DIM))
            s_ki.append(to_tm(s32s[:, :IDX_DIM].reshape(t_new, db, IDX_DIM)))
        wup16, cwr, cbr, wdn16 = prep_ffn_weights(w_up[l], ffn_conv_w[l], ffn_conv_b[l], w_down[l], FFN_CHUNK)
        xp, fcp = ffn(xp, norm_ffn[l], wup16, cwr, cbr, wdn16, jnp.zeros((bsz, 2, 2 * D_FF), F32),
                      n_seq=bsz, dil=1, tm=FFN_TM, fc=FFN_CHUNK)
        pre_t = arrange_cols(to_tm(state_ffn_conv[l]).reshape(1, 2 * db, 2 * D_FF), FFN_CHUNK)
        xs, fcs = ffn(xs, norm_ffn[l], wup16, cwr, cbr, wdn16, pre_t, n_seq=1, dil=db, tm=t_new * db, fc=FFN_CHUNK)
        p_ffn.append(unarrange_cols(fcp[tiles_per_seq - 1::tiles_per_seq], FFN_CHUNK))
        s_ffn.append(to_tm(unarrange_cols(fcs, FFN_CHUNK).reshape(2, db, 2 * D_FF)))
    y_prompt = rmsnorm_call(xp, norm_final, tm=PROMPT_TM).reshape(bsz, seq, d)
    y_sample = to_tm(rmsnorm_call(xs, norm_final, tm=PROMPT_TM).reshape(t_new, db, d))
    return (y_prompt, y_sample,
            jnp.stack(p_conv), jnp.stack(p_hg), jnp.stack(p_k), jnp.stack(p_v), jnp.stack(p_ki), jnp.stack(p_ffn),
            jnp.stack(s_conv), jnp.stack(s_hg), jnp.stack(s_k), jnp.stack(s_v), jnp.stack(s_ki), jnp.stack(s_ffn))
```

```python
import functools
import math

import jax
import jax.numpy as jnp
from jax import lax
from jax.experimental import pallas as pl
from jax.experimental.pallas import tpu as pltpu

F32 = jnp.float32
BF16 = jnp.bfloat16
I32 = jnp.int32

EPS = 1e-6
D_MODEL = 1024
A_WIDTH = 512
A_CONV = 31
B_HEADS = 4
B_DK = 128
B_WIDTH = 512
C_HEADS = 8
C_HEAD_DIM = 128
C_WIDTH = 1024
IDX_HEADS = 8
IDX_DIM = 64
TOPK = 256
REL_BUCKETS = 32
REL_MAX_DIST = 128
D_FF = 2816
PAGE = 128

LANES = 128
VMEM_LIMIT = 60 * 1024 * 1024
NEG_BIG = -1e30
INT_MIN = -2 ** 31


def _cparams(sem):
    return pltpu.CompilerParams(dimension_semantics=sem, vmem_limit_bytes=VMEM_LIMIT)


def _rms(x, g):
    return x * lax.rsqrt(jnp.mean(x * x, axis=-1, keepdims=True) + EPS) * g


def _sigmoid(x):
    return 1.0 / (1.0 + jnp.exp(-x))


def _dot(a, b):
    return jnp.dot(a, b, preferred_element_type=F32)


def _dot_nt(a, b):
    return lax.dot_general(a, b, (((1,), (1,)), ((), ())), preferred_element_type=F32)


def _norm_matmul_kernel(x_ref, g_ref, w_ref, *out_refs, segs):
    hb = _rms(x_ref[...], g_ref[...]).astype(BF16)
    for start, width, scale, outs in segs:
        for c0 in range(0, width, 512):
            cw = min(512, width - c0)
            r = _dot(hb, w_ref[:, start + c0:start + c0 + cw])
            if scale != 1.0:
                r = r * scale
            for oi in outs:
                out_refs[oi][:, c0:c0 + cw] = r.astype(out_refs[oi].dtype)


def norm_matmul(x, g, w16, segs, out_defs, tm=512):
    m, d = x.shape
    tm = min(tm, m)
    n = w16.shape[1]
    return pl.pallas_call(
        functools.partial(_norm_matmul_kernel, segs=segs),
        grid=(m // tm,),
        in_specs=[pl.BlockSpec((tm, d), lambda i: (i, 0)),
                  pl.BlockSpec((1, d), lambda i: (0, 0)),
                  pl.BlockSpec((d, n), lambda i: (0, 0))],
        out_specs=[pl.BlockSpec((tm, wd), lambda i: (i, 0)) for wd, _ in out_defs],
        out_shape=[jax.ShapeDtypeStruct((m, wd), dt) for wd, dt in out_defs],
        compiler_params=_cparams(("arbitrary",)),
        name="norm_matmul",
    )(x, g.reshape(1, d), w16)


def _outproj_kernel(*refs, n_in):
    x_ref = refs[0]
    o_ref = refs[-1]
    acc = x_ref[...]
    for i in range(n_in):
        acc = acc + _dot(refs[1 + 2 * i][...].astype(BF16), refs[2 + 2 * i][...])
    o_ref[...] = acc


def outproj(x, ms, ws, tm=512):
    m, d = x.shape
    tm = min(tm, m)
    in_specs = [pl.BlockSpec((tm, d), lambda i: (i, 0))]
    args = [x]
    for mm, w in zip(ms, ws):
        in_specs.append(pl.BlockSpec((tm, mm.shape[1]), lambda i: (i, 0)))
        in_specs.append(pl.BlockSpec(w.shape, lambda i: (0, 0)))
        args += [mm, w]
    return pl.pallas_call(
        functools.partial(_outproj_kernel, n_in=len(ms)),
        grid=(m // tm,),
        in_specs=in_specs,
        out_specs=pl.BlockSpec((tm, d), lambda i: (i, 0)),
        out_shape=jax.ShapeDtypeStruct((m, d), F32),
        compiler_params=_cparams(("arbitrary",)),
        name="outproj",
    )(*args)


FFN_SUB = 256


def _ffn_kernel(x_ref, g_ref, wup_ref, cw_ref, cb_ref, wdn_ref, pre_ref, o_ref, npre_ref,
                ubuf_s, *, tm, dil, pad, tiles_per_seq):
    first = (pl.program_id(0) % tiles_per_seq) == 0

    @pl.when(first)
    def _():
        ubuf_s[pad - 2 * dil:pad, :] = pre_ref[0]

    sub = min(tm, FFN_SUB)
    for r in range(tm // sub):
        r0 = pad + r * sub
        xr = x_ref[r * sub:(r + 1) * sub, :]
        u = _dot(_rms(xr, g_ref[...]).astype(BF16), wup_ref[...])
        ubuf_s[r0:r0 + sub, :] = u
        uc = (u * cw_ref[2:3, :] + ubuf_s[r0 - dil:r0 - dil + sub, :] * cw_ref[1:2, :]
              + ubuf_s[r0 - 2 * dil:r0 - 2 * dil + sub, :] * cw_ref[0:1, :] + cb_ref[...])
        gate = uc[:, :D_FF]
        val = uc[:, D_FF:]
        act = (gate * _sigmoid(gate) * val).astype(BF16)
        o_ref[r * sub:(r + 1) * sub, :] = xr + _dot(act, wdn_ref[...])
    tail = ubuf_s[pad + tm - 2 * dil:pad + tm, :]
    npre_ref[0] = tail
    if tiles_per_seq > 1:
        ubuf_s[pad - 2 * dil:pad, :] = tail


def ffn(x, g, wup16, cw, cb, wdn16, pre, *, n_seq, dil, tm):
    m, d = x.shape
    f2 = 2 * D_FF
    tiles_per_seq = (m // n_seq) // tm
    pad = max(8, 2 * dil)
    kern = functools.partial(_ffn_kernel, tm=tm, dil=dil, pad=pad, tiles_per_seq=tiles_per_seq)
    once = pl.Buffered(1)
    return pl.pallas_call(
        kern,
        grid=(m // tm,),
        in_specs=[pl.BlockSpec((tm, d), lambda i: (i, 0)),
                  pl.BlockSpec((1, d), lambda i: (0, 0)),
                  pl.BlockSpec((d, f2), lambda i: (0, 0), pipeline_mode=once),
                  pl.BlockSpec((3, f2), lambda i: (0, 0)),
                  pl.BlockSpec((1, f2), lambda i: (0, 0)),
                  pl.BlockSpec((D_FF, d), lambda i: (0, 0), pipeline_mode=once),
                  pl.BlockSpec((1, 2 * dil, f2), lambda i: (i // tiles_per_seq, 0, 0))],
        out_specs=[pl.BlockSpec((tm, d), lambda i: (i, 0)),
                   pl.BlockSpec((1, 2 * dil, f2), lambda i: (i, 0, 0))],
        out_shape=[jax.ShapeDtypeStruct((m, d), F32),
                   jax.ShapeDtypeStruct((m // tm, 2 * dil, f2), F32)],
        scratch_shapes=[pltpu.VMEM((pad + tm, f2), F32)],
        compiler_params=_cparams(("arbitrary",)),
        name="conv_ffn",
    )(x, g.reshape(1, d), wup16, cw, cb.reshape(1, f2), wdn16, pre)


def _rmsnorm_kernel(x_ref, g_ref, o_ref):
    o_ref[...] = _rms(x_ref[...], g_ref[...])


def rmsnorm_call(x, g, tm=512):
    m, d = x.shape
    tm = min(tm, m)
    return pl.pallas_call(
        _rmsnorm_kernel,
        grid=(m // tm,),
        in_specs=[pl.BlockSpec((tm, d), lambda i: (i, 0)), pl.BlockSpec((1, d), lambda i: (0, 0))],
        out_specs=pl.BlockSpec((tm, d), lambda i: (i, 0)),
        out_shape=jax.ShapeDtypeStruct((m, d), F32),
        compiler_params=_cparams(("arbitrary",)),
        name="final_rmsnorm",
    )(x, g.reshape(1, d))


def _mixer_a_kernel(val_ref, gate_ref, cw_ref, cb_ref, lg_ref, lb_ref, pre_ref, o_ref, npre_ref,
                    gbuf_s, *, tm, dil, pad, tiles_per_seq):
    i = pl.program_id(0)
    first = (i % tiles_per_seq) == 0
    hist = (A_CONV - 1) * dil

    @pl.when(first)
    def _():
        gbuf_s[pad - hist:pad, :] = pre_ref[0]

    glu = val_ref[...] * _sigmoid(gate_ref[...])
    gbuf_s[pad:pad + tm, :] = glu
    npre_ref[0] = gbuf_s[pad + tm - hist:pad + tm, :]
    acc = jnp.zeros((tm, A_WIDTH), F32) + cb_ref[...]
    for j in range(A_CONV):
        s0 = pad - hist + j * dil
        acc = acc + gbuf_s[s0:s0 + tm, :] * cw_ref[j:j + 1, :]
    mu = jnp.mean(acc, axis=-1, keepdims=True)
    xc = acc - mu
    var = jnp.mean(xc * xc, axis=-1, keepdims=True)
    y = xc * lax.rsqrt(var + EPS) * lg_ref[...] + lb_ref[...]
    o_ref[...] = (y * _sigmoid(y)).astype(o_ref.dtype)
    if tiles_per_seq > 1:
        gbuf_s[0:pad, :] = gbuf_s[tm:tm + pad, :]


def mixer_a(z, cw, cb, lg, lb, pre, *, n_seq, dil, tm):
    m = z.shape[0]
    tiles_per_seq = (m // n_seq) // tm
    hist = (A_CONV - 1) * dil
    pad = 32 if dil == 1 else hist
    kern = functools.partial(_mixer_a_kernel, tm=tm, dil=dil, pad=pad, tiles_per_seq=tiles_per_seq)
    vec = lambda a: a.reshape(1, A_WIDTH)
    return pl.pallas_call(
        kern,
        grid=(m // tm,),
        in_specs=[pl.BlockSpec((tm, A_WIDTH), lambda i: (i, 0)),
                  pl.BlockSpec((tm, A_WIDTH), lambda i: (i, 1)),
                  pl.BlockSpec((A_CONV, A_WIDTH), lambda i: (0, 0)),
                  pl.BlockSpec((1, A_WIDTH), lambda i: (0, 0)),
                  pl.BlockSpec((1, A_WIDTH), lambda i: (0, 0)),
                  pl.BlockSpec((1, A_WIDTH), lambda i: (0, 0)),
                  pl.BlockSpec((1, hist, A_WIDTH), lambda i: (i // tiles_per_seq, 0, 0))],
        out_specs=[pl.BlockSpec((tm, A_WIDTH), lambda i: (i, 0)),
                   pl.BlockSpec((1, hist, A_WIDTH), lambda i: (i // tiles_per_seq, 0, 0))],
        out_shape=[jax.ShapeDtypeStruct((m, A_WIDTH), BF16),
                   jax.ShapeDtypeStruct((n_seq, hist, A_WIDTH), F32)],
        scratch_shapes=[pltpu.VMEM((pad + tm, A_WIDTH), F32)],
        compiler_params=_cparams(("arbitrary",)),
        name="mixer_a",
    )(z, z, cw, vec(cb), vec(lg), vec(lb), pre)


HG_CHUNK = 128
HG_SUB = 16


def _split3(x):
    hi = x.astype(BF16)
    r1 = x - hi.astype(F32)
    mid = r1.astype(BF16)
    lo = (r1 - mid.astype(F32)).astype(BF16)
    return hi, mid, lo


def _hgrn_chunk(qr, fr, ir, gr, llb, l1m, gn, st_s, b_s, q_s, k_s, v_s, oacc_s, *, n_real, nsb):
    c = HG_CHUNK
    q = qr * _sigmoid(qr)
    ls = jnp.minimum(fr, 0.0) - jnp.log1p(jnp.exp(-jnp.abs(fr)))
    bt = l1m + ls
    logf = jnp.maximum(llb, bt) + jnp.log1p(jnp.exp(-jnp.abs(llb - bt)))
    kk = 1.0 - jnp.exp(logf)
    if n_real < c:
        real = lax.broadcasted_iota(I32, (c, B_WIDTH), 0) < n_real
        logf = jnp.where(real, logf, 0.0)
        kk = jnp.where(real, kk, 0.0)
    ri = lax.broadcasted_iota(I32, (c, c), 0)
    ci = lax.broadcasted_iota(I32, (c, c), 1)
    tril = jnp.where(ci <= ri, 1.0, 0.0).astype(BF16)
    hi, mid, lo = _split3(logf)
    b = _dot(tril, hi) + _dot(tril, mid) + _dot(tril, lo)
    b_s[...] = b
    q_s[...] = q
    k_s[...] = kk
    v_s[...] = ir
    sub = HG_SUB
    row16 = lax.broadcasted_iota(I32, (sub, 1), 0)
    outs = []
    for h in range(B_HEADS):
        sl = slice(h * B_DK, (h + 1) * B_DK)
        st = st_s[h]
        oacc_s[:, sl] = _dot_nt((q[:, sl] * jnp.exp(b[:, sl])).astype(BF16), st.astype(BF16))
        for i in range(nsb):
            r0 = i * sub
            qb = q_s[r0:r0 + sub, sl]
            bb = b_s[r0:r0 + sub, sl]
            o_blk = jnp.zeros((sub, B_DK), F32)
            if i > 0:
                ref_b = b_s[r0 - 1:r0, sl]
                qt = (qb * jnp.exp(bb - ref_b)).astype(BF16)
                kt = (k_s[0:r0, sl] * jnp.exp(ref_b - b_s[0:r0, sl])).astype(BF16)
                att = _dot_nt(qt, kt)
                o_blk = o_blk + _dot(att.astype(BF16), v_s[0:r0, sl].astype(BF16))
            for s in range(sub):
                r = r0 + s
                if r >= n_real:
                    continue
                p = qb * jnp.exp(bb - b_s[r:r + 1, sl]) * k_s[r:r + 1, sl]
                a = jnp.sum(p, axis=1, keepdims=True)
                a = jnp.where(row16 >= s, a, 0.0)
                o_blk = o_blk + a * v_s[r:r + 1, sl]
            oacc_s[r0:r0 + sub, sl] += o_blk
        o = oacc_s[:, sl]
        on = o * lax.rsqrt(jnp.mean(o * o, axis=-1, keepdims=True) + EPS) * gn
        g = gr[:, sl]
        outs.append(on * (g * _sigmoid(g)))
        bl = b_s[c - 1:c, sl]
        kd = (k_s[:, sl] * jnp.exp(bl - b_s[:, sl])).astype(BF16)
        vt = v_s[:, sl].T.astype(BF16)
        st_s[h] = st * jnp.exp(bl) + _dot(vt, kd)
    return jnp.concatenate(outs, axis=1)


def _hgrn_kernel(q_ref, f_ref, i_ref, g_ref, llb_ref, l1m_ref, gn_ref, s0_ref, o_ref, sout_ref,
                 st_s, b_s, q_s, k_s, v_s, oacc_s, *, rt, n_real, nsb, tiles_per_seq):
    c = HG_CHUNK
    first = (pl.program_id(0) % tiles_per_seq) == 0

    @pl.when(first)
    def _():
        for h in range(B_HEADS):
            st_s[h] = s0_ref[0, h].T

    rows = min(rt, c)
    for ch in range(max(1, rt // c)):
        rs = slice(ch * c, ch * c + rows)

        def load(ref):
            x = ref[rs, :]
            if rows < c:
                x = jnp.concatenate([x, jnp.zeros((c - rows, B_WIDTH), F32)], axis=0)
            return x

        out = _hgrn_chunk(load(q_ref), load(f_ref), load(i_ref), load(g_ref),
                          llb_ref[...], l1m_ref[...], gn_ref[...],
                          st_s, b_s, q_s, k_s, v_s, oacc_s, n_real=n_real, nsb=nsb)
        o_ref[rs, :] = out[0:rows, :].astype(o_ref.dtype)
    for h in range(B_HEADS):
        sout_ref[0, h] = st_s[h].T


def hgrn(z, col0, llb, l1m, gn, s0, *, n_seq, rt, n_real, nsb):
    m = z.shape[0]
    tiles_per_seq = (m // n_seq) // rt
    kern = functools.partial(_hgrn_kernel, rt=rt, n_real=n_real, nsb=nsb, tiles_per_seq=tiles_per_seq)
    c = HG_CHUNK
    zspec = lambda k: pl.BlockSpec((rt, B_WIDTH), lambda i: (i, col0 + k))
    state_spec = pl.BlockSpec((1, B_HEADS, B_DK, B_DK), lambda i: (i // tiles_per_seq, 0, 0, 0))
    return pl.pallas_call(
        kern,
        grid=(m // rt,),
        in_specs=[zspec(0), zspec(1), zspec(2), zspec(3),
                  pl.BlockSpec((1, B_WIDTH), lambda i: (0, 0)),
                  pl.BlockSpec((1, B_WIDTH), lambda i: (0, 0)),
                  pl.BlockSpec((1, B_DK), lambda i: (0, 0)),
                  state_spec],
        out_specs=[pl.BlockSpec((rt, B_WIDTH), lambda i: (i, 0)), state_spec],
        out_shape=[jax.ShapeDtypeStruct((m, B_WIDTH), BF16),
                   jax.ShapeDtypeStruct((n_seq, B_HEADS, B_DK, B_DK), F32)],
        scratch_shapes=[pltpu.VMEM((B_HEADS, B_DK, B_DK), F32),
                        pltpu.VMEM((c, B_WIDTH), F32),
                        pltpu.VMEM((c, B_WIDTH), F32),
                        pltpu.VMEM((c, B_WIDTH), F32),
                        pltpu.VMEM((c, B_WIDTH), F32),
                        pltpu.VMEM((c, B_WIDTH), F32)],
        compiler_params=_cparams(("arbitrary",)),
        name="hgrn2",
    )(z, z, z, z, llb.reshape(1, B_WIDTH), l1m.reshape(1, B_WIDTH), gn.reshape(1, B_DK), s0)


def _order_key(x):
    b = lax.bitcast_convert_type(x + 0.0, I32)
    return jnp.where(b < 0, b ^ 0x7FFFFFFF, b)


def _t5_bucket(dist):
    n = jnp.maximum(dist, 0)
    max_exact = REL_BUCKETS // 2
    nf = jnp.maximum(n, 1).astype(F32)
    large = max_exact + (jnp.log(nf / max_exact) / math.log(REL_MAX_DIST / max_exact)
                         * (REL_BUCKETS - max_exact)).astype(I32)
    large = jnp.minimum(large, REL_BUCKETS - 1)
    return jnp.where(n < max_exact, n, large)


def _bias_table(dist, rb_ref, h):
    bucket = _t5_bucket(dist)
    last = rb_ref[REL_BUCKETS - 1, h]
    out = jnp.zeros(dist.shape, F32)
    for bk in range(REL_BUCKETS - 1):
        out = jnp.where(bucket == bk, rb_ref[bk, h] - last, out)
    return out


LOG2E = math.log2(math.e)
ATT_ROWS = 128
I16 = jnp.int16
I16_MIN = -2 ** 15
SRCH_ROWS = 64
SRCH_W = 256
SRCH_KEYS = 2048


def _wide(x, n):
    return jnp.concatenate([x] * (n // LANES), axis=1)


def _dsa_prompt_kernel(q_ref, qi_ref, s_ref, k_ref, v_ref, kit_ref, rb_ref, o_ref,
                       hi_s, lo_s, wb_s, bias_s, m_s, l_s, acc_s, j_s, mb_s, cand_s, thi_s, tlo_s, cnt_s,
                       *, tq, nbits, sk):
    b = pl.program_id(0)
    i = pl.program_id(1)
    row = lax.broadcasted_iota(I32, (tq, tq), 0)
    col = lax.broadcasted_iota(I32, (tq, tq), 1)

    @pl.when((b == 0) & (i == 0))
    def _():
        for h in range(C_HEADS):
            bias_s[0, h] = _bias_table(row + tq - col, rb_ref, h) * LOG2E
            bias_s[1, h] = _bias_table(row - col, rb_ref, h) * LOG2E

    @pl.when(i == 0)
    def _():
        hi_s[...] = jnp.full(hi_s.shape, I16_MIN, I16)
        lo_s[...] = jnp.full(lo_s.shape, I16_MIN, I16)

    wscale = IDX_DIM ** -0.5 * IDX_HEADS ** -0.5
    for h in range(IDX_HEADS):
        wcol = s_ref[:, IDX_DIM + h:IDX_DIM + h + 1] * wscale
        wb_s[h] = jnp.broadcast_to(wcol, (tq, LANES))

    def score_tile(j, carry):
        off = pl.multiple_of(j * tq, tq)
        kt = kit_ref[0, :, pl.ds(off, tq)]
        sc = jnp.zeros((tq, tq), F32)
        for h in range(IDX_HEADS):
            d = _dot(qi_ref[:, h * LANES:(h + 1) * LANES], kt)
            sc = sc + jnp.maximum(d, 0.0) * _wide(wb_s[h], tq)
        sc = jnp.where((j == i) & (col > row), -jnp.inf, sc)
        key = _order_key(sc)
        hi_s[:, pl.ds(off, tq)] = lax.shift_right_arithmetic(key, 16).astype(I16)
        lo_s[:, pl.ds(off, tq)] = ((key & 0xFFFF) + I16_MIN).astype(I16)
        return carry

    lax.fori_loop(0, i + 1, score_tile, 0)

    n_steps = ((i + 1) * tq + sk - 1) // sk
    ones = jnp.ones((SRCH_W, LANES), BF16)

    def scan(fn, init=None):
        for rc in range(tq // SRCH_ROWS):
            rows = slice(rc * SRCH_ROWS, (rc + 1) * SRCH_ROWS)

            def body(j, acc):
                for q in range(sk // SRCH_W):
                    off = pl.multiple_of(j * sk, sk) + q * SRCH_W
                    acc = fn(rows, off, acc)
                return acc

            out = lax.fori_loop(0, n_steps, body, init)
            if init is not None:
                cnt_s[rows, :] = out

    def count(buf, pred):
        def fn(rows, off, acc):
            return acc + jnp.where(pred(buf[rows, pl.ds(off, SRCH_W)], cand_s[rows, :]), BF16(1), BF16(0))
        scan(fn, jnp.zeros((SRCH_ROWS, SRCH_W), BF16))
        return _dot(cnt_s[...], ones)

    def search(buf, base, n_init):
        def bit_step(bi, carry):
            thr, n_ge = carry
            cand = thr + lax.shift_left(jnp.int32(1), 15 - bi)
            cand_s[...] = _wide(cand.astype(I16), SRCH_W)
            cnt = count(buf, lambda kq, c: kq >= c) + base
            ok = cnt >= TOPK
            return jnp.where(ok, cand, thr), jnp.where(ok, cnt, n_ge)
        return lax.fori_loop(0, 16, bit_step, (jnp.full((tq, LANES), I16_MIN, I32), n_init))

    n_all = jnp.zeros((tq, LANES), F32) + ((i + 1) * tq).astype(F32)
    thi, n_ge_hi = search(hi_s, 0.0, n_all)
    thi_s[...] = _wide(thi.astype(I16), SRCH_W)
    cand_s[...] = thi_s[...]
    n_gt_hi = count(hi_s, lambda kq, c: kq > c)

    def keep_tied_hi(rows, off, acc):
        tied = hi_s[rows, pl.ds(off, SRCH_W)] == thi_s[rows, :]
        lo_s[rows, pl.ds(off, SRCH_W)] = jnp.where(tied, lo_s[rows, pl.ds(off, SRCH_W)], I16(I16_MIN))
        return acc

    scan(keep_tied_hi)
    tlo, n_ge = search(lo_s, n_gt_hi, n_ge_hi)
    tlo_s[...] = _wide(tlo.astype(I16), SRCH_W)
    cand_s[...] = tlo_s[...]
    n_gt = count(lo_s, lambda kq, c: kq > c) + n_gt_hi

    need = TOPK - n_gt
    j_s[...] = jnp.full((tq, SRCH_W), 2 ** 14, I16)
    any_tie = jnp.max(jnp.where(n_ge > TOPK, 1.0, 0.0)) > 0.0

    @pl.when(any_tie)
    def _():
        lane16 = lax.broadcasted_iota(I32, (SRCH_ROWS, SRCH_W), 1)

        def tied_below(rows, off, acc):
            tied = ((hi_s[rows, pl.ds(off, SRCH_W)] == thi_s[rows, :])
                    & (lo_s[rows, pl.ds(off, SRCH_W)] == tlo_s[rows, :])
                    & ((lane16 + off).astype(I16) < cand_s[rows, :]))
            return acc + jnp.where(tied, BF16(1), BF16(0))

        def idx_step(bi, jv):
            cand = jv + lax.shift_left(jnp.int32(1), nbits - 1 - bi)
            cand_s[...] = _wide(cand.astype(I16), SRCH_W)
            scan(tied_below, jnp.zeros((SRCH_ROWS, SRCH_W), BF16))
            cnt = _dot(cnt_s[...], ones)
            return jnp.where(cnt < need, cand, jv)

        jv = lax.fori_loop(0, nbits, idx_step, jnp.zeros((tq, LANES), I32))
        j_s[...] = _wide(jv.astype(I16), SRCH_W)

    m_s[...] = jnp.full(m_s.shape, NEG_BIG, F32)
    l_s[...] = jnp.zeros(l_s.shape, F32)
    acc_s[...] = jnp.zeros(acc_s.shape, F32)
    rowg16 = (row + i * tq).astype(I16)

    def att_tile(j, tab):
        off = pl.multiple_of(j * tq, tq)
        kh16 = hi_s[:, pl.ds(off, tq)]
        kl16 = lo_s[:, pl.ds(off, tq)]
        colg16 = (col + off).astype(I16)
        thi16, tlo16 = thi_s[...], tlo_s[...]
        mask = ((kh16 > thi16) | ((kh16 == thi16) & ((kl16 > tlo16) | ((kl16 == tlo16) & (colg16 <= j_s[...])))))
        mask = mask & (colg16 <= rowg16)
        mb_s[...] = jnp.where(mask, I16(0), I16(-1)).astype(F32) * (-NEG_BIG)
        for h in range(C_HEADS):
            hs = slice(h * C_HEAD_DIM, (h + 1) * C_HEAD_DIM)
            kh = k_ref[0, pl.ds(off, tq), hs]
            vh = v_ref[0, pl.ds(off, tq), hs]
            for rc in range(tq // ATT_ROWS):
                rs = slice(rc * ATT_ROWS, (rc + 1) * ATT_ROWS)
                s = _dot_nt(q_ref[rs, hs], kh) + mb_s[rs, :]
                if tab is not None:
                    s = s + bias_s[tab, h, rs, :]
                m_prev = m_s[h, rs, :]
                m_new = jnp.maximum(m_prev, jnp.max(s, axis=1, keepdims=True))
                alpha = jnp.exp2(m_prev - m_new)
                p = jnp.exp2(s - _wide(m_new, tq))
                l_s[h, rs, :] = alpha * l_s[h, rs, :] + jnp.sum(p, axis=1, keepdims=True)
                acc_s[h, rs, :] = alpha * acc_s[h, rs, :] + _dot(p.astype(BF16), vh)
                m_s[h, rs, :] = m_new

    def far_tile(j, carry):
        att_tile(j, None)
        return carry

    lax.fori_loop(0, jnp.maximum(i - 1, 0), far_tile, 0)

    @pl.when(i >= 1)
    def _():
        att_tile(i - 1, 0)

    att_tile(i, 1)
    for h in range(C_HEADS):
        o_ref[:, h * C_HEAD_DIM:(h + 1) * C_HEAD_DIM] = (acc_s[h] / l_s[h]).astype(o_ref.dtype)


def dsa_prompt(q16, qi16, s32, k16, v16, kit16, rel_bias, *, n_seq, tq=256):
    m = q16.shape[0]
    seq = m // n_seq
    nq = seq // tq
    nbits = int(math.log2(seq))
    assert 2 ** nbits == seq and seq % tq == 0
    assert tq == SRCH_W and tq >= TOPK
    kern = functools.partial(_dsa_prompt_kernel, tq=tq, nbits=nbits, sk=min(SRCH_KEYS, seq))
    once = pl.Buffered(1)
    return pl.pallas_call(
        kern,
        grid=(n_seq, nq),
        in_specs=[pl.BlockSpec((tq, C_WIDTH), lambda b, i: (b * nq + i, 0)),
                  pl.BlockSpec((tq, IDX_HEADS * LANES), lambda b, i: (b * nq + i, 0)),
                  pl.BlockSpec((tq, LANES), lambda b, i: (b * nq + i, 0)),
                  pl.BlockSpec((1, seq, C_WIDTH), lambda b, i: (b, 0, 0), pipeline_mode=once),
                  pl.BlockSpec((1, seq, C_WIDTH), lambda b, i: (b, 0, 0), pipeline_mode=once),
                  pl.BlockSpec((1, LANES, seq), lambda b, i: (b, 0, 0), pipeline_mode=once),
                  pl.BlockSpec(memory_space=pltpu.SMEM)],
        out_specs=pl.BlockSpec((tq, C_WIDTH), lambda b, i: (b * nq + i, 0)),
        out_shape=jax.ShapeDtypeStruct((m, C_WIDTH), BF16),
        scratch_shapes=[pltpu.VMEM((tq, seq), I16),
                        pltpu.VMEM((tq, seq), I16),
                        pltpu.VMEM((IDX_HEADS, tq, LANES), F32),
                        pltpu.VMEM((2, C_HEADS, tq, tq), F32),
                        pltpu.VMEM((C_HEADS, tq, LANES), F32),
                        pltpu.VMEM((C_HEADS, tq, LANES), F32),
                        pltpu.VMEM((C_HEADS, tq, C_HEAD_DIM), F32),
                        pltpu.VMEM((tq, SRCH_W), I16),
                        pltpu.VMEM((tq, tq), F32),
                        pltpu.VMEM((tq, SRCH_W), I16),
                        pltpu.VMEM((tq, SRCH_W), I16),
                        pltpu.VMEM((tq, SRCH_W), I16),
                        pltpu.VMEM((tq, SRCH_W), BF16)],
        compiler_params=_cparams(("arbitrary", "arbitrary")),
        name="dsa_prompt",
    )(q16, qi16, s32, k16, v16, kit16, rel_bias)


NEW_ROWS = 16


def _dsa_sample_kernel(pt_ref, *refs, n_pages, t_new):
    ki_refs = refs[0:n_pages]
    k_refs = refs[n_pages:2 * n_pages]
    v_refs = refs[2 * n_pages:3 * n_pages]
    (qi_ref, wrep_ref, q_ref, kin_ref, knew_ref, vnew_ref, rb_ref, o_ref,
     kiall_s, xk_s, xv_s, lg_s, tab_s, exp_s) = refs[3 * n_pages:]
    b = pl.program_id(0)
    past = n_pages * PAGE
    nkeys = past + PAGE
    n_tiles = n_pages + 1
    nth = t_new * C_HEADS
    pw = PAGE * C_HEADS
    rowx = lax.broadcasted_iota(I32, (nth, pw), 0)
    lanex = lax.broadcasted_iota(I32, (nth, pw), 1)
    own_head = (lanex % C_HEADS) == (rowx % C_HEADS)

    @pl.when(b == 0)
    def _():
        kiall_s[past:nkeys, :] = jnp.zeros((PAGE, LANES), BF16)
        xk_s[past * C_HEADS:nkeys * C_HEADS, :] = jnp.zeros((pw, C_HEAD_DIM), BF16)
        xv_s[past * C_HEADS:nkeys * C_HEADS, :] = jnp.zeros((pw, C_HEAD_DIM), BF16)
        ei = lax.broadcasted_iota(I32, (PAGE, pw), 0)
        ej = lax.broadcasted_iota(I32, (PAGE, pw), 1)
        exp_s[...] = jnp.where(ej // C_HEADS == ei, 1.0, 0.0).astype(BF16)
        t_of = rowx // C_HEADS
        k_of = lanex // C_HEADS
        tab0 = jnp.zeros((nth, pw), F32)
        tab1 = jnp.zeros((nth, pw), F32)
        for h in range(C_HEADS):
            sel = (lanex % C_HEADS) == h
            tab0 = jnp.where(sel, _bias_table(PAGE + t_of - k_of, rb_ref, h), tab0)
            tab1 = jnp.where(sel, _bias_table(t_of - k_of, rb_ref, h), tab1)
        tab_s[0] = tab0 * LOG2E
        tab_s[1] = tab1 * LOG2E

    zpad = jnp.zeros((PAGE, LANES - IDX_DIM), F32)
    for p in range(n_pages):
        kiall_s[p * PAGE:(p + 1) * PAGE, :] = jnp.concatenate([ki_refs[p][0, 0], zpad], axis=1).astype(BF16)
        xk_s[p * pw:(p + 1) * pw, :] = k_refs[p][0, 0].reshape(pw, C_HEAD_DIM).astype(BF16)
        xv_s[p * pw:(p + 1) * pw, :] = v_refs[p][0, 0].reshape(pw, C_HEAD_DIM).astype(BF16)
    kiall_s[past:past + NEW_ROWS, :] = jnp.concatenate(
        [kin_ref[0], jnp.zeros((NEW_ROWS, LANES - IDX_DIM), F32)], axis=1).astype(BF16)
    xk_s[past * C_HEADS:(past + NEW_ROWS) * C_HEADS, :] = knew_ref[0]
    xv_s[past * C_HEADS:(past + NEW_ROWS) * C_HEADS, :] = vnew_ref[0]

    d = _dot_nt(qi_ref[0], kiall_s[...])
    r = jnp.maximum(d, 0.0) * _wide(wrep_ref[0], nkeys)
    sc = jnp.sum(r.reshape(t_new, C_HEADS, nkeys), axis=1)
    rowr = lax.broadcasted_iota(I32, (t_new, nkeys), 0)
    colr = lax.broadcasted_iota(I32, (t_new, nkeys), 1)
    valid = (colr < past) | (colr - past <= rowr)
    kr = _order_key(jnp.where(valid, sc, -jnp.inf))

    def pick(ok1, c1, ok2, c2, ok3, c3, cur):
        return jnp.where(ok3, c3, jnp.where(ok2, c2, jnp.where(ok1, c1, cur)))

    def bit_step(bi, thr):
        one = lax.shift_left(jnp.int32(1), 30 - 2 * bi)
        c1, c2, c3 = thr + one, thr + 2 * one, thr + 3 * one
        cnt = lambda c: jnp.sum(jnp.where(kr >= c, 1, 0), axis=1, keepdims=True)
        return pick(cnt(c1) >= TOPK, c1, cnt(c2) >= TOPK, c2, cnt(c3) >= TOPK, c3, thr)

    thr = lax.fori_loop(0, 16, bit_step, jnp.full((t_new, 1), INT_MIN, I32))
    need = TOPK - jnp.sum(jnp.where(kr > thr, 1, 0), axis=1, keepdims=True)
    tie = kr == thr
    nsteps = (int(math.ceil(math.log2(nkeys))) + 1) // 2

    def idx_step(bi, jv):
        one = lax.shift_left(jnp.int32(1), 2 * (nsteps - 1 - bi))
        c1, c2, c3 = jv + one, jv + 2 * one, jv + 3 * one
        cnt = lambda c: jnp.sum(jnp.where(tie & (colr < c), 1, 0), axis=1, keepdims=True)
        return pick(cnt(c1) < need, c1, cnt(c2) < need, c2, cnt(c3) < need, c3, jv)

    jv = lax.fori_loop(0, nsteps, idx_step, jnp.zeros((t_new, 1), I32))
    sel = ((kr > thr) | (tie & (colr <= jv))) & valid
    sel = jnp.broadcast_to(jnp.where(sel, 1.0, 0.0)[:, None, :], (t_new, C_HEADS, nkeys))
    sel16 = sel.reshape(nth, nkeys).astype(BF16)

    qs = q_ref[0]
    sel_rows = jnp.concatenate([sel16[:, p * PAGE:(p + 1) * PAGE] for p in range(n_tiles)], axis=0)
    selx = _dot(sel_rows, exp_s[...])
    for p in range(n_tiles):
        s = _dot_nt(qs, xk_s[p * pw:(p + 1) * pw, :])
        if p >= n_pages - 1:
            s = s + tab_s[p - (n_pages - 1)]
        lg_s[p] = jnp.where((selx[p * nth:(p + 1) * nth, :] > 0.5) & own_head, s, NEG_BIG)
    mx = jnp.full((nth, 1), NEG_BIG, F32)
    for p in range(n_tiles):
        mx = jnp.maximum(mx, jnp.max(lg_s[p], axis=1, keepdims=True))
    den = jnp.zeros((nth, 1), F32)
    out = jnp.zeros((nth, C_HEAD_DIM), F32)
    for p in range(n_tiles):
        pr = jnp.exp2(lg_s[p] - mx)
        den = den + jnp.sum(pr, axis=1, keepdims=True)
        out = out + _dot(pr.astype(BF16), xv_s[p * pw:(p + 1) * pw, :])
    o_ref[0] = out / den


def sample_dsa_operands(q16, qi16, s32, k16, v16, t_new, db):
    nth = t_new * C_HEADS
    by_seq = lambda a: a.reshape((t_new, db) + a.shape[1:]).swapaxes(0, 1)
    qi = by_seq(qi16.reshape(t_new * db, IDX_HEADS, LANES)).reshape(db, nth, LANES)
    q = by_seq(q16.reshape(t_new * db, C_HEADS, C_HEAD_DIM)).reshape(db, nth, C_HEAD_DIM)
    wscale = IDX_DIM ** -0.5 * IDX_HEADS ** -0.5
    w = by_seq(s32[:, IDX_DIM:IDX_DIM + IDX_HEADS]).reshape(db, nth, 1) * wscale
    wrep = jnp.broadcast_to(w, (db, nth, LANES))
    row_pad = ((0, 0), (0, NEW_ROWS - t_new), (0, 0))
    kin = jnp.pad(by_seq(s32[:, :IDX_DIM]), row_pad)
    new_rows = lambda a: jnp.pad(by_seq(a), row_pad).reshape(db, NEW_ROWS * C_HEADS, C_HEAD_DIM)
    return qi, wrep, q, kin, new_rows(k16), new_rows(v16)


def dsa_sample(page_table, layer, pool_ki, pool_k, pool_v, qi16, wrep, q16, kin, knew16, vnew16, rel_bias, *, t_new):
    db, n_pages = page_table.shape
    n_tiles = n_pages + 1
    nkeys = n_tiles * PAGE
    nth = t_new * C_HEADS
    pw = PAGE * C_HEADS
    kern = functools.partial(_dsa_sample_kernel, n_pages=n_pages, t_new=t_new)

    def ki_spec(p):
        return pl.BlockSpec((1, 1, PAGE, IDX_DIM), lambda b, pt: (layer, pt[b, p], 0, 0))

    def kv_spec(p):
        return pl.BlockSpec((1, 1, PAGE, C_HEADS, C_HEAD_DIM), lambda b, pt: (layer, pt[b, p], 0, 0, 0))

    per_seq = lambda shape: pl.BlockSpec((1,) + shape, lambda b, pt: (b, 0, 0))
    in_specs = ([ki_spec(p) for p in range(n_pages)]
                + [kv_spec(p) for p in range(n_pages)]
                + [kv_spec(p) for p in range(n_pages)]
                + [per_seq((nth, LANES)), per_seq((nth, LANES)), per_seq((nth, C_HEAD_DIM)),
                   per_seq((NEW_ROWS, IDX_DIM)), per_seq((NEW_ROWS * C_HEADS, C_HEAD_DIM)),
                   per_seq((NEW_ROWS * C_HEADS, C_HEAD_DIM)),
                   pl.BlockSpec(memory_space=pltpu.SMEM)])
    grid_spec = pltpu.PrefetchScalarGridSpec(
        num_scalar_prefetch=1,
        grid=(db,),
        in_specs=in_specs,
        out_specs=pl.BlockSpec((1, nth, C_HEAD_DIM), lambda b, pt: (b, 0, 0)),
        scratch_shapes=[pltpu.VMEM((nkeys, LANES), BF16),
                        pltpu.VMEM((nkeys * C_HEADS, C_HEAD_DIM), BF16),
                        pltpu.VMEM((nkeys * C_HEADS, C_HEAD_DIM), BF16),
                        pltpu.VMEM((n_tiles, nth, pw), F32),
                        pltpu.VMEM((2, nth, pw), F32),
                        pltpu.VMEM((PAGE, pw), BF16)])
    return pl.pallas_call(
        kern,
        grid_spec=grid_spec,
        out_shape=jax.ShapeDtypeStruct((db, nth, C_HEAD_DIM), F32),
        compiler_params=_cparams(("arbitrary",)),
        name="dsa_sample",
    )(page_table, *([pool_ki] * n_pages), *([pool_k] * n_pages), *([pool_v] * n_pages),
      qi16, wrep, q16, kin, knew16, vnew16, rel_bias)


PROMPT_TM = 512
FFN_TM = 512
ODD_COLS = 3 * C_WIDTH + IDX_HEADS * LANES + LANES


def _odd_weight(w):
    d = w.shape[0]
    qkv = w[:, :3 * C_WIDTH]
    qi = w[:, 3 * C_WIDTH:3 * C_WIDTH + IDX_HEADS * IDX_DIM].reshape(d, IDX_HEADS, IDX_DIM)
    qi = jnp.pad(qi, ((0, 0), (0, 0), (0, LANES - IDX_DIM))).reshape(d, IDX_HEADS * LANES)
    tail = w[:, 3 * C_WIDTH + IDX_HEADS * IDX_DIM:]
    tail = jnp.pad(tail, ((0, 0), (0, LANES - tail.shape[1])))
    return jnp.concatenate([qkv, qi, tail], axis=1).astype(BF16)


ODD_SEGS = ((0, C_WIDTH, C_HEAD_DIM ** -0.5 * LOG2E, (0,)),
            (C_WIDTH, C_WIDTH, 1.0, (1, 2)),
            (2 * C_WIDTH, C_WIDTH, 1.0, (3, 4)),
            (3 * C_WIDTH, IDX_HEADS * LANES, 1.0, (5,)),
            (3 * C_WIDTH + IDX_HEADS * LANES, LANES, 1.0, (6,)))
ODD_OUTS = ((C_WIDTH, BF16), (C_WIDTH, F32), (C_WIDTH, BF16), (C_WIDTH, F32), (C_WIDTH, BF16),
            (IDX_HEADS * LANES, BF16), (LANES, F32))
EVEN_COLS = 2 * A_WIDTH + 4 * B_WIDTH


def kernel(x_prompt, x_sample, state_conv_a, state_hgrn, cache_k, cache_v, cache_ki, state_ffn_conv, page_table, norm_mix, norm_ffn, norm_final, w_in_even, w_in_odd, w_out, conv_a_w, conv_a_b, ln_a_g, ln_a_b, hgrn_lb, hgrn_gn, rel_bias, w_up, ffn_conv_w, ffn_conv_b, w_down):
    bsz, seq, d = x_prompt.shape
    db, t_new, _ = x_sample.shape
    depth = norm_mix.shape[0]
    n_phys = cache_k.shape[1]
    to_tm = lambda a: jnp.transpose(a, (1, 0, 2))

    xp = x_prompt.reshape(bsz * seq, d)
    xs = to_tm(x_sample).reshape(t_new * db, d)
    lbs = jnp.cumsum(jax.nn.softmax(hgrn_lb.astype(F32), axis=0), axis=0)
    lbs = lbs - lbs[0:1]
    log_lb = jnp.log(lbs)
    log_1m_lb = jnp.log1p(-lbs)
    tiles_per_seq = seq // FFN_TM

    p_conv, p_hg, p_k, p_v, p_ki, p_ffn = [], [], [], [], [], []
    s_conv, s_hg, s_k, s_v, s_ki, s_ffn = [], [], [], [], [], []
    for l in range(depth):
        j = l // 2
        wo16 = w_out[l].astype(BF16)
        if l % 2 == 0:
            w16 = w_in_even[j].astype(BF16)
            segs = ((0, EVEN_COLS, 1.0, (0,)),)
            (zp,) = norm_matmul(xp, norm_mix[l], w16, segs, ((EVEN_COLS, F32),), tm=PROMPT_TM)
            (zs,) = norm_matmul(xs, norm_mix[l], w16, segs, ((EVEN_COLS, F32),), tm=PROMPT_TM)
            conv = (conv_a_w[j], conv_a_b[j], ln_a_g[j], ln_a_b[j])
            ap, cp = mixer_a(zp, *conv, jnp.zeros((bsz, A_CONV - 1, A_WIDTH), F32), n_seq=bsz, dil=1, tm=PROMPT_TM)
            bp, sp = hgrn(zp, 2, log_lb[j], log_1m_lb[j], hgrn_gn[j],
                          jnp.zeros((bsz, B_HEADS, B_DK, B_DK), F32),
                          n_seq=bsz, rt=HG_CHUNK, n_real=HG_CHUNK, nsb=HG_CHUNK // HG_SUB)
            pre_t = to_tm(state_conv_a[j]).reshape(1, (A_CONV - 1) * db, A_WIDTH)
            a_s, cs = mixer_a(zs, *conv, pre_t, n_seq=1, dil=db, tm=t_new * db)
            zb = to_tm(zs.reshape(t_new, db, EVEN_COLS))[:, :, 2 * A_WIDTH:]
            zb = jnp.pad(zb, ((0, 0), (0, 8 - t_new), (0, 0))).reshape(db * 8, 4 * B_WIDTH)
            bs8, ss = hgrn(zb, 0, log_lb[j], log_1m_lb[j], hgrn_gn[j], state_hgrn[j],
                           n_seq=db, rt=8, n_real=t_new, nsb=1)
            b_s = to_tm(bs8.reshape(db, 8, B_WIDTH)[:, :t_new]).reshape(t_new * db, B_WIDTH)
            xp = outproj(xp, [ap, bp], [wo16[:A_WIDTH], wo16[A_WIDTH:]], tm=PROMPT_TM)
            xs = outproj(xs, [a_s, b_s], [wo16[:A_WIDTH], wo16[A_WIDTH:]], tm=PROMPT_TM)
            p_conv.append(cp)
            p_hg.append(sp)
            s_conv.append(to_tm(cs.reshape(A_CONV - 1, db, A_WIDTH)))
            s_hg.append(ss)
        else:
            w16 = _odd_weight(w_in_odd[j])
            q16, k32, k16, v32, v16, qi16, s32 = norm_matmul(xp, norm_mix[l], w16, ODD_SEGS, ODD_OUTS, tm=PROMPT_TM)
            kit16 = jnp.pad(jnp.transpose(s32[:, :IDX_DIM].reshape(bsz, seq, IDX_DIM), (0, 2, 1)),
                            ((0, 0), (0, LANES - IDX_DIM), (0, 0))).astype(BF16)
            mp = dsa_prompt(q16, qi16, s32, k16.reshape(bsz, seq, C_WIDTH), v16.reshape(bsz, seq, C_WIDTH),
                            kit16, rel_bias, n_seq=bsz)
            p_k.append(k32.reshape(bsz, seq, C_HEADS, C_HEAD_DIM))
            p_v.append(v32.reshape(bsz, seq, C_HEADS, C_HEAD_DIM))
            p_ki.append(s32[:, :IDX_DIM].reshape(bsz, seq, IDX_DIM))
            q16s, k32s, k16s, v32s, v16s, qi16s, s32s = norm_matmul(xs, norm_mix[l], w16, ODD_SEGS, ODD_OUTS,
                                                                   tm=PROMPT_TM)
            ops = sample_dsa_operands(q16s, qi16s, s32s, k16s, v16s, t_new, db)
            o_s = dsa_sample(page_table, j, cache_ki, cache_k, cache_v, *ops, rel_bias, t_new=t_new)
            m_s = to_tm(o_s.reshape(db, t_new, C_WIDTH)).reshape(t_new * db, C_WIDTH)
            xp = outproj(xp, [mp], [wo16], tm=PROMPT_TM)
            xs = outproj(xs, [m_s], [wo16], tm=PROMPT_TM)
            s_k.append(to_tm(k32s.reshape(t_new, db, C_WIDTH)).reshape(db, t_new, C_HEADS, C_HEAD_DIM))
            s_v.append(to_tm(v32s.reshape(t_new, db, C_WIDTH)).reshape(db, t_new, C_HEADS, C_HEAD_DIM))
            s_ki.append(to_tm(s32s[:, :IDX_DIM].reshape(t_new, db, IDX_DIM)))
        ffn_w = (w_up[l].astype(BF16), ffn_conv_w[l], ffn_conv_b[l], w_down[l].astype(BF16))
        xp, fcp = ffn(xp, norm_ffn[l], *ffn_w, jnp.zeros((bsz, 2, 2 * D_FF), F32), n_seq=bsz, dil=1, tm=FFN_TM)
        pre_t = to_tm(state_ffn_conv[l]).reshape(1, 2 * db, 2 * D_FF)
        xs, fcs = ffn(xs, norm_ffn[l], *ffn_w, pre_t, n_seq=1, dil=db, tm=t_new * db)
        p_ffn.append(fcp[tiles_per_seq - 1::tiles_per_seq])
        s_ffn.append(to_tm(fcs.reshape(2, db, 2 * D_FF)))
    y_prompt = rmsnorm_call(xp, norm_final, tm=PROMPT_TM).reshape(bsz, seq, d)
    y_sample = to_tm(rmsnorm_call(xs, norm_final, tm=PROMPT_TM).reshape(t_new, db, d))
    return (y_prompt, y_sample,
            jnp.stack(p_conv), jnp.stack(p_hg), jnp.stack(p_k), jnp.stack(p_v), jnp.stack(p_ki), jnp.stack(p_ffn),
            jnp.stack(s_conv), jnp.stack(s_hg), jnp.stack(s_k), jnp.stack(s_v), jnp.stack(s_ki), jnp.stack(s_ffn))
```

```python
import functools
import math

import jax
import jax.numpy as jnp
from jax import lax
from jax.experimental import pallas as pl
from jax.experimental.pallas import tpu as pltpu

F32 = jnp.float32
BF16 = jnp.bfloat16
I32 = jnp.int32

EPS = 1e-6
D_MODEL = 1024
A_WIDTH = 512
A_CONV = 31
B_HEADS = 4
B_DK = 128
B_WIDTH = 512
C_HEADS = 8
C_HEAD_DIM = 128
C_WIDTH = 1024
IDX_HEADS = 8
IDX_DIM = 64
TOPK = 256
REL_BUCKETS = 32
REL_MAX_DIST = 128
D_FF = 2816
PAGE = 128

LANES = 128
VMEM_LIMIT = 60 * 1024 * 1024
NEG_BIG = -1e30
INT_MIN = -2 ** 31


def _cparams(sem):
    return pltpu.CompilerParams(dimension_semantics=sem, vmem_limit_bytes=VMEM_LIMIT)


def _rms(x, g):
    return x * lax.rsqrt(jnp.mean(x * x, axis=-1, keepdims=True) + EPS) * g


def _sigmoid(x):
    return 1.0 / (1.0 + jnp.exp(-x))


def _dot(a, b):
    return jnp.dot(a, b, preferred_element_type=F32)


def _dot_nt(a, b):
    return lax.dot_general(a, b, (((1,), (1,)), ((), ())), preferred_element_type=F32)


def _norm_matmul_kernel(x_ref, g_ref, w_ref, *out_refs, segs):
    hb = _rms(x_ref[...], g_ref[...]).astype(BF16)
    for start, width, scale, outs in segs:
        for c0 in range(0, width, 512):
            cw = min(512, width - c0)
            r = _dot(hb, w_ref[:, start + c0:start + c0 + cw])
            if scale != 1.0:
                r = r * scale
            for oi in outs:
                out_refs[oi][:, c0:c0 + cw] = r.astype(out_refs[oi].dtype)


def norm_matmul(x, g, w16, segs, out_defs, tm=512):
    m, d = x.shape
    tm = min(tm, m)
    n = w16.shape[1]
    return pl.pallas_call(
        functools.partial(_norm_matmul_kernel, segs=segs),
        grid=(m // tm,),
        in_specs=[pl.BlockSpec((tm, d), lambda i: (i, 0)),
                  pl.BlockSpec((1, d), lambda i: (0, 0)),
                  pl.BlockSpec((d, n), lambda i: (0, 0))],
        out_specs=[pl.BlockSpec((tm, wd), lambda i: (i, 0)) for wd, _ in out_defs],
        out_shape=[jax.ShapeDtypeStruct((m, wd), dt) for wd, dt in out_defs],
        compiler_params=_cparams(("arbitrary",)),
        name="norm_matmul",
    )(x, g.reshape(1, d), w16)


def _outproj_kernel(*refs, n_in):
    x_ref = refs[0]
    o_ref = refs[-1]
    acc = x_ref[...]
    for i in range(n_in):
        acc = acc + _dot(refs[1 + 2 * i][...].astype(BF16), refs[2 + 2 * i][...])
    o_ref[...] = acc


def outproj(x, ms, ws, tm=512):
    m, d = x.shape
    tm = min(tm, m)
    in_specs = [pl.BlockSpec((tm, d), lambda i: (i, 0))]
    args = [x]
    for mm, w in zip(ms, ws):
        in_specs.append(pl.BlockSpec((tm, mm.shape[1]), lambda i: (i, 0)))
        in_specs.append(pl.BlockSpec(w.shape, lambda i: (0, 0)))
        args += [mm, w]
    return pl.pallas_call(
        functools.partial(_outproj_kernel, n_in=len(ms)),
        grid=(m // tm,),
        in_specs=in_specs,
        out_specs=pl.BlockSpec((tm, d), lambda i: (i, 0)),
        out_shape=jax.ShapeDtypeStruct((m, d), F32),
        compiler_params=_cparams(("arbitrary",)),
        name="outproj",
    )(*args)


FFN_SUB = 256


def _ffn_kernel(x_ref, g_ref, wup_ref, cw_ref, cb_ref, wdn_ref, pre_ref, o_ref, npre_ref,
                ubuf_s, *, tm, dil, pad, tiles_per_seq):
    first = (pl.program_id(0) % tiles_per_seq) == 0

    @pl.when(first)
    def _():
        ubuf_s[pad - 2 * dil:pad, :] = pre_ref[0]

    sub = min(tm, FFN_SUB)
    for r in range(tm // sub):
        r0 = pad + r * sub
        xr = x_ref[r * sub:(r + 1) * sub, :]
        u = _dot(_rms(xr, g_ref[...]).astype(BF16), wup_ref[...])
        ubuf_s[r0:r0 + sub, :] = u
        uc = (u * cw_ref[2:3, :] + ubuf_s[r0 - dil:r0 - dil + sub, :] * cw_ref[1:2, :]
              + ubuf_s[r0 - 2 * dil:r0 - 2 * dil + sub, :] * cw_ref[0:1, :] + cb_ref[...])
        gate = uc[:, :D_FF]
        val = uc[:, D_FF:]
        act = (gate * _sigmoid(gate) * val).astype(BF16)
        o_ref[r * sub:(r + 1) * sub, :] = xr + _dot(act, wdn_ref[...])
    tail = ubuf_s[pad + tm - 2 * dil:pad + tm, :]
    npre_ref[0] = tail
    if tiles_per_seq > 1:
        ubuf_s[pad - 2 * dil:pad, :] = tail


def ffn(x, g, wup16, cw, cb, wdn16, pre, *, n_seq, dil, tm):
    m, d = x.shape
    f2 = 2 * D_FF
    tiles_per_seq = (m // n_seq) // tm
    pad = max(8, 2 * dil)
    kern = functools.partial(_ffn_kernel, tm=tm, dil=dil, pad=pad, tiles_per_seq=tiles_per_seq)
    once = pl.Buffered(1)
    return pl.pallas_call(
        kern,
        grid=(m // tm,),
        in_specs=[pl.BlockSpec((tm, d), lambda i: (i, 0)),
                  pl.BlockSpec((1, d), lambda i: (0, 0)),
                  pl.BlockSpec((d, f2), lambda i: (0, 0), pipeline_mode=once),
                  pl.BlockSpec((3, f2), lambda i: (0, 0)),
                  pl.BlockSpec((1, f2), lambda i: (0, 0)),
                  pl.BlockSpec((D_FF, d), lambda i: (0, 0), pipeline_mode=once),
                  pl.BlockSpec((1, 2 * dil, f2), lambda i: (i // tiles_per_seq, 0, 0))],
        out_specs=[pl.BlockSpec((tm, d), lambda i: (i, 0)),
                   pl.BlockSpec((1, 2 * dil, f2), lambda i: (i, 0, 0))],
        out_shape=[jax.ShapeDtypeStruct((m, d), F32),
                   jax.ShapeDtypeStruct((m // tm, 2 * dil, f2), F32)],
        scratch_shapes=[pltpu.VMEM((pad + tm, f2), F32)],
        compiler_params=_cparams(("arbitrary",)),
        name="conv_ffn",
    )(x, g.reshape(1, d), wup16, cw, cb.reshape(1, f2), wdn16, pre)


def _rmsnorm_kernel(x_ref, g_ref, o_ref):
    o_ref[...] = _rms(x_ref[...], g_ref[...])


def rmsnorm_call(x, g, tm=512):
    m, d = x.shape
    tm = min(tm, m)
    return pl.pallas_call(
        _rmsnorm_kernel,
        grid=(m // tm,),
        in_specs=[pl.BlockSpec((tm, d), lambda i: (i, 0)), pl.BlockSpec((1, d), lambda i: (0, 0))],
        out_specs=pl.BlockSpec((tm, d), lambda i: (i, 0)),
        out_shape=jax.ShapeDtypeStruct((m, d), F32),
        compiler_params=_cparams(("arbitrary",)),
        name="final_rmsnorm",
    )(x, g.reshape(1, d))


def _mixer_a_kernel(val_ref, gate_ref, cw_ref, cb_ref, lg_ref, lb_ref, pre_ref, o_ref, npre_ref,
                    gbuf_s, *, tm, dil, pad, tiles_per_seq):
    i = pl.program_id(0)
    first = (i % tiles_per_seq) == 0
    hist = (A_CONV - 1) * dil

    @pl.when(first)
    def _():
        gbuf_s[pad - hist:pad, :] = pre_ref[0]

    glu = val_ref[...] * _sigmoid(gate_ref[...])
    gbuf_s[pad:pad + tm, :] = glu
    npre_ref[0] = gbuf_s[pad + tm - hist:pad + tm, :]
    acc = jnp.zeros((tm, A_WIDTH), F32) + cb_ref[...]
    for j in range(A_CONV):
        s0 = pad - hist + j * dil
        acc = acc + gbuf_s[s0:s0 + tm, :] * cw_ref[j:j + 1, :]
    mu = jnp.mean(acc, axis=-1, keepdims=True)
    xc = acc - mu
    var = jnp.mean(xc * xc, axis=-1, keepdims=True)
    y = xc * lax.rsqrt(var + EPS) * lg_ref[...] + lb_ref[...]
    o_ref[...] = (y * _sigmoid(y)).astype(o_ref.dtype)
    if tiles_per_seq > 1:
        gbuf_s[0:pad, :] = gbuf_s[tm:tm + pad, :]


def mixer_a(z, cw, cb, lg, lb, pre, *, n_seq, dil, tm):
    m = z.shape[0]
    tiles_per_seq = (m // n_seq) // tm
    hist = (A_CONV - 1) * dil
    pad = 32 if dil == 1 else hist
    kern = functools.partial(_mixer_a_kernel, tm=tm, dil=dil, pad=pad, tiles_per_seq=tiles_per_seq)
    vec = lambda a: a.reshape(1, A_WIDTH)
    return pl.pallas_call(
        kern,
        grid=(m // tm,),
        in_specs=[pl.BlockSpec((tm, A_WIDTH), lambda i: (i, 0)),
                  pl.BlockSpec((tm, A_WIDTH), lambda i: (i, 1)),
                  pl.BlockSpec((A_CONV, A_WIDTH), lambda i: (0, 0)),
                  pl.BlockSpec((1, A_WIDTH), lambda i: (0, 0)),
                  pl.BlockSpec((1, A_WIDTH), lambda i: (0, 0)),
                  pl.BlockSpec((1, A_WIDTH), lambda i: (0, 0)),
                  pl.BlockSpec((1, hist, A_WIDTH), lambda i: (i // tiles_per_seq, 0, 0))],
        out_specs=[pl.BlockSpec((tm, A_WIDTH), lambda i: (i, 0)),
                   pl.BlockSpec((1, hist, A_WIDTH), lambda i: (i // tiles_per_seq, 0, 0))],
        out_shape=[jax.ShapeDtypeStruct((m, A_WIDTH), BF16),
                   jax.ShapeDtypeStruct((n_seq, hist, A_WIDTH), F32)],
        scratch_shapes=[pltpu.VMEM((pad + tm, A_WIDTH), F32)],
        compiler_params=_cparams(("arbitrary",)),
        name="mixer_a",
    )(z, z, cw, vec(cb), vec(lg), vec(lb), pre)


HG_CHUNK = 128
HG_SUB = 16


def _split3(x):
    hi = x.astype(BF16)
    r1 = x - hi.astype(F32)
    mid = r1.astype(BF16)
    lo = (r1 - mid.astype(F32)).astype(BF16)
    return hi, mid, lo


def _hgrn_chunk(qr, fr, ir, gr, llb, l1m, gn, st_s, b_s, q_s, k_s, v_s, oacc_s, *, n_real, nsb):
    c = HG_CHUNK
    q = qr * _sigmoid(qr)
    ls = jnp.minimum(fr, 0.0) - jnp.log1p(jnp.exp(-jnp.abs(fr)))
    bt = l1m + ls
    logf = jnp.maximum(llb, bt) + jnp.log1p(jnp.exp(-jnp.abs(llb - bt)))
    kk = 1.0 - jnp.exp(logf)
    if n_real < c:
        real = lax.broadcasted_iota(I32, (c, B_WIDTH), 0) < n_real
        logf = jnp.where(real, logf, 0.0)
        kk = jnp.where(real, kk, 0.0)
    ri = lax.broadcasted_iota(I32, (c, c), 0)
    ci = lax.broadcasted_iota(I32, (c, c), 1)
    tril = jnp.where(ci <= ri, 1.0, 0.0).astype(BF16)
    hi, mid, lo = _split3(logf)
    b = _dot(tril, hi) + _dot(tril, mid) + _dot(tril, lo)
    b_s[...] = b
    q_s[...] = q
    k_s[...] = kk
    v_s[...] = ir
    sub = HG_SUB
    row16 = lax.broadcasted_iota(I32, (sub, 1), 0)
    outs = []
    for h in range(B_HEADS):
        sl = slice(h * B_DK, (h + 1) * B_DK)
        st = st_s[h]
        oacc_s[:, sl] = _dot_nt((q[:, sl] * jnp.exp(b[:, sl])).astype(BF16), st.astype(BF16))
        for i in range(nsb):
            r0 = i * sub
            qb = q_s[r0:r0 + sub, sl]
            bb = b_s[r0:r0 + sub, sl]
            o_blk = jnp.zeros((sub, B_DK), F32)
            if i > 0:
                ref_b = b_s[r0 - 1:r0, sl]
                qt = (qb * jnp.exp(bb - ref_b)).astype(BF16)
                kt = (k_s[0:r0, sl] * jnp.exp(ref_b - b_s[0:r0, sl])).astype(BF16)
                att = _dot_nt(qt, kt)
                o_blk = o_blk + _dot(att.astype(BF16), v_s[0:r0, sl].astype(BF16))
            for s in range(sub):
                r = r0 + s
                if r >= n_real:
                    continue
                p = qb * jnp.exp(bb - b_s[r:r + 1, sl]) * k_s[r:r + 1, sl]
                a = jnp.sum(p, axis=1, keepdims=True)
                a = jnp.where(row16 >= s, a, 0.0)
                o_blk = o_blk + a * v_s[r:r + 1, sl]
            oacc_s[r0:r0 + sub, sl] += o_blk
        o = oacc_s[:, sl]
        on = o * lax.rsqrt(jnp.mean(o * o, axis=-1, keepdims=True) + EPS) * gn
        g = gr[:, sl]
        outs.append(on * (g * _sigmoid(g)))
        bl = b_s[c - 1:c, sl]
        kd = (k_s[:, sl] * jnp.exp(bl - b_s[:, sl])).astype(BF16)
        vt = v_s[:, sl].T.astype(BF16)
        st_s[h] = st * jnp.exp(bl) + _dot(vt, kd)
    return jnp.concatenate(outs, axis=1)


def _hgrn_kernel(q_ref, f_ref, i_ref, g_ref, llb_ref, l1m_ref, gn_ref, s0_ref, o_ref, sout_ref,
                 st_s, b_s, q_s, k_s, v_s, oacc_s, *, rt, n_real, nsb, tiles_per_seq):
    c = HG_CHUNK
    first = (pl.program_id(0) % tiles_per_seq) == 0

    @pl.when(first)
    def _():
        for h in range(B_HEADS):
            st_s[h] = s0_ref[0, h].T

    rows = min(rt, c)
    for ch in range(max(1, rt // c)):
        rs = slice(ch * c, ch * c + rows)

        def load(ref):
            x = ref[rs, :]
            if rows < c:
                x = jnp.concatenate([x, jnp.zeros((c - rows, B_WIDTH), F32)], axis=0)
            return x

        out = _hgrn_chunk(load(q_ref), load(f_ref), load(i_ref), load(g_ref),
                          llb_ref[...], l1m_ref[...], gn_ref[...],
                          st_s, b_s, q_s, k_s, v_s, oacc_s, n_real=n_real, nsb=nsb)
        o_ref[rs, :] = out[0:rows, :].astype(o_ref.dtype)
    for h in range(B_HEADS):
        sout_ref[0, h] = st_s[h].T


def hgrn(z, col0, llb, l1m, gn, s0, *, n_seq, rt, n_real, nsb):
    m = z.shape[0]
    tiles_per_seq = (m // n_seq) // rt
    kern = functools.partial(_hgrn_kernel, rt=rt, n_real=n_real, nsb=nsb, tiles_per_seq=tiles_per_seq)
    c = HG_CHUNK
    zspec = lambda k: pl.BlockSpec((rt, B_WIDTH), lambda i: (i, col0 + k))
    state_spec = pl.BlockSpec((1, B_HEADS, B_DK, B_DK), lambda i: (i // tiles_per_seq, 0, 0, 0))
    return pl.pallas_call(
        kern,
        grid=(m // rt,),
        in_specs=[zspec(0), zspec(1), zspec(2), zspec(3),
                  pl.BlockSpec((1, B_WIDTH), lambda i: (0, 0)),
                  pl.BlockSpec((1, B_WIDTH), lambda i: (0, 0)),
                  pl.BlockSpec((1, B_DK), lambda i: (0, 0)),
                  state_spec],
        out_specs=[pl.BlockSpec((rt, B_WIDTH), lambda i: (i, 0)), state_spec],
        out_shape=[jax.ShapeDtypeStruct((m, B_WIDTH), BF16),
                   jax.ShapeDtypeStruct((n_seq, B_HEADS, B_DK, B_DK), F32)],
        scratch_shapes=[pltpu.VMEM((B_HEADS, B_DK, B_DK), F32),
                        pltpu.VMEM((c, B_WIDTH), F32),
                        pltpu.VMEM((c, B_WIDTH), F32),
                        pltpu.VMEM((c, B_WIDTH), F32),
                        pltpu.VMEM((c, B_WIDTH), F32),
                        pltpu.VMEM((c, B_WIDTH), F32)],
        compiler_params=_cparams(("arbitrary",)),
        name="hgrn2",
    )(z, z, z, z, llb.reshape(1, B_WIDTH), l1m.reshape(1, B_WIDTH), gn.reshape(1, B_DK), s0)


def _order_key(x):
    b = lax.bitcast_convert_type(x + 0.0, I32)
    return jnp.where(b < 0, b ^ 0x7FFFFFFF, b)


def _t5_bucket(dist):
    n = jnp.maximum(dist, 0)
    max_exact = REL_BUCKETS // 2
    nf = jnp.maximum(n, 1).astype(F32)
    large = max_exact + (jnp.log(nf / max_exact) / math.log(REL_MAX_DIST / max_exact)
                         * (REL_BUCKETS - max_exact)).astype(I32)
    large = jnp.minimum(large, REL_BUCKETS - 1)
    return jnp.where(n < max_exact, n, large)


def _bias_table(dist, rb_ref, h):
    bucket = _t5_bucket(dist)
    last = rb_ref[REL_BUCKETS - 1, h]
    out = jnp.zeros(dist.shape, F32)
    for bk in range(REL_BUCKETS - 1):
        out = jnp.where(bucket == bk, rb_ref[bk, h] - last, out)
    return out


LOG2E = math.log2(math.e)
ATT_COLS = 128
I16 = jnp.int16
I16_MIN = -2 ** 15
SRCH_ROWS = 64
SRCH_W = 256
SRCH_KEYS = 2048


def _wide(x, n):
    return jnp.concatenate([x] * (n // LANES), axis=1)


def _dsa_prompt_kernel(q_ref, qi_ref, s_ref, k_ref, vt_ref, kit_ref, rb_ref, o_ref,
                       hi_s, lo_s, wb_s, bias_s, m_s, l_s, acc_s, j_s, mb_s, cand_s, thi_s, tlo_s, cnt_s,
                       *, tq, nbits, sk):
    b = pl.program_id(0)
    i = pl.program_id(1)
    row = lax.broadcasted_iota(I32, (tq, tq), 0)
    col = lax.broadcasted_iota(I32, (tq, tq), 1)

    @pl.when((b == 0) & (i == 0))
    def _():
        for h in range(C_HEADS):
            bias_s[0, h] = _bias_table(col + tq - row, rb_ref, h) * LOG2E
            bias_s[1, h] = _bias_table(col - row, rb_ref, h) * LOG2E

    @pl.when(i == 0)
    def _():
        hi_s[...] = jnp.full(hi_s.shape, I16_MIN, I16)
        lo_s[...] = jnp.full(lo_s.shape, I16_MIN, I16)

    wscale = IDX_DIM ** -0.5 * IDX_HEADS ** -0.5
    for h in range(IDX_HEADS):
        wcol = s_ref[:, IDX_DIM + h:IDX_DIM + h + 1] * wscale
        wb_s[h] = jnp.broadcast_to(wcol, (tq, LANES))

    def score_tile(j, carry):
        off = pl.multiple_of(j * tq, tq)
        kt = kit_ref[0, :, pl.ds(off, tq)]
        sc = jnp.zeros((tq, tq), F32)
        for h in range(IDX_HEADS):
            d = _dot(qi_ref[:, h * LANES:(h + 1) * LANES], kt)
            sc = sc + jnp.maximum(d, 0.0) * _wide(wb_s[h], tq)
        sc = jnp.where((j == i) & (col > row), -jnp.inf, sc)
        key = _order_key(sc)
        hi_s[:, pl.ds(off, tq)] = lax.shift_right_arithmetic(key, 16).astype(I16)
        lo_s[:, pl.ds(off, tq)] = ((key & 0xFFFF) + I16_MIN).astype(I16)
        return carry

    lax.fori_loop(0, i + 1, score_tile, 0)

    n_steps = ((i + 1) * tq + sk - 1) // sk
    ones = jnp.ones((SRCH_W, LANES), BF16)

    def scan(fn, init=None):
        for rc in range(tq // SRCH_ROWS):
            rows = slice(rc * SRCH_ROWS, (rc + 1) * SRCH_ROWS)

            def body(j, acc):
                for q in range(sk // SRCH_W):
                    off = pl.multiple_of(j * sk, sk) + q * SRCH_W
                    acc = fn(rows, off, acc)
                return acc

            out = lax.fori_loop(0, n_steps, body, init)
            if init is not None:
                cnt_s[rows, :] = out

    def count(buf, pred):
        def fn(rows, off, acc):
            return acc + jnp.where(pred(buf[rows, pl.ds(off, SRCH_W)], cand_s[rows, :]), BF16(1), BF16(0))
        scan(fn, jnp.zeros((SRCH_ROWS, SRCH_W), BF16))
        return _dot(cnt_s[...], ones)

    def search(buf, base, n_init):
        def bit_step(bi, carry):
            thr, n_ge = carry
            cand = thr + lax.shift_left(jnp.int32(1), 15 - bi)
            cand_s[...] = _wide(cand.astype(I16), SRCH_W)
            cnt = count(buf, lambda kq, c: kq >= c) + base
            ok = cnt >= TOPK
            return jnp.where(ok, cand, thr), jnp.where(ok, cnt, n_ge)
        return lax.fori_loop(0, 16, bit_step, (jnp.full((tq, LANES), I16_MIN, I32), n_init))

    n_all = jnp.zeros((tq, LANES), F32) + ((i + 1) * tq).astype(F32)
    thi, n_ge_hi = search(hi_s, 0.0, n_all)
    thi_s[...] = _wide(thi.astype(I16), SRCH_W)
    cand_s[...] = thi_s[...]
    n_gt_hi = count(hi_s, lambda kq, c: kq > c)

    def keep_tied_hi(rows, off, acc):
        tied = hi_s[rows, pl.ds(off, SRCH_W)] == thi_s[rows, :]
        lo_s[rows, pl.ds(off, SRCH_W)] = jnp.where(tied, lo_s[rows, pl.ds(off, SRCH_W)], I16(I16_MIN))
        return acc

    scan(keep_tied_hi)
    tlo, n_ge = search(lo_s, n_gt_hi, n_ge_hi)
    tlo_s[...] = _wide(tlo.astype(I16), SRCH_W)
    cand_s[...] = tlo_s[...]
    n_gt = count(lo_s, lambda kq, c: kq > c) + n_gt_hi

    need = TOPK - n_gt
    j_s[...] = jnp.full((tq, SRCH_W), 2 ** 14, I16)
    any_tie = jnp.max(jnp.where(n_ge > TOPK, 1.0, 0.0)) > 0.0

    @pl.when(any_tie)
    def _():
        lane16 = lax.broadcasted_iota(I32, (SRCH_ROWS, SRCH_W), 1)

        def tied_below(rows, off, acc):
            tied = ((hi_s[rows, pl.ds(off, SRCH_W)] == thi_s[rows, :])
                    & (lo_s[rows, pl.ds(off, SRCH_W)] == tlo_s[rows, :])
                    & ((lane16 + off).astype(I16) < cand_s[rows, :]))
            return acc + jnp.where(tied, BF16(1), BF16(0))

        def idx_step(bi, jv):
            cand = jv + lax.shift_left(jnp.int32(1), nbits - 1 - bi)
            cand_s[...] = _wide(cand.astype(I16), SRCH_W)
            scan(tied_below, jnp.zeros((SRCH_ROWS, SRCH_W), BF16))
            cnt = _dot(cnt_s[...], ones)
            return jnp.where(cnt < need, cand, jv)

        jv = lax.fori_loop(0, nbits, idx_step, jnp.zeros((tq, LANES), I32))
        j_s[...] = _wide(jv.astype(I16), SRCH_W)

    m_s[...] = jnp.full(m_s.shape, NEG_BIG, F32)
    l_s[...] = jnp.zeros(l_s.shape, F32)
    acc_s[...] = jnp.zeros(acc_s.shape, F32)
    rowg16 = (row + i * tq).astype(I16)

    def att_tile(j, tab):
        off = pl.multiple_of(j * tq, tq)
        kh16 = hi_s[:, pl.ds(off, tq)]
        kl16 = lo_s[:, pl.ds(off, tq)]
        colg16 = (col + off).astype(I16)
        thi16, tlo16 = thi_s[...], tlo_s[...]
        mask = ((kh16 > thi16) | ((kh16 == thi16) & ((kl16 > tlo16) | ((kl16 == tlo16) & (colg16 <= j_s[...])))))
        mask = mask & (colg16 <= rowg16)
        mb_s[...] = (jnp.where(mask, I16(0), I16(-1)).astype(F32) * (-NEG_BIG)).T
        for h in range(C_HEADS):
            hs = slice(h * C_HEAD_DIM, (h + 1) * C_HEAD_DIM)
            kh = k_ref[0, pl.ds(off, tq), hs]
            vth = vt_ref[0, hs, pl.ds(off, tq)]
            for qc in range(tq // ATT_COLS):
                qs = slice(qc * ATT_COLS, (qc + 1) * ATT_COLS)
                s = _dot_nt(kh, q_ref[qs, hs]) + mb_s[:, qs]
                if tab is not None:
                    s = s + bias_s[tab, h, :, qs]
                m_prev = m_s[h, :, qs]
                m_new = jnp.maximum(m_prev, jnp.max(s, axis=0, keepdims=True))
                alpha = jnp.exp2(m_prev - m_new)
                p = jnp.exp2(s - m_new)
                l_s[h, :, qs] = alpha * l_s[h, :, qs] + jnp.sum(p, axis=0, keepdims=True)
                acc_s[h, :, qs] = alpha * acc_s[h, :, qs] + _dot(vth, p.astype(BF16))
                m_s[h, :, qs] = m_new

    def far_tile(j, carry):
        att_tile(j, None)
        return carry

    lax.fori_loop(0, jnp.maximum(i - 1, 0), far_tile, 0)

    @pl.when(i >= 1)
    def _():
        att_tile(i - 1, 0)

    att_tile(i, 1)
    for h in range(C_HEADS):
        o_ref[:, h * C_HEAD_DIM:(h + 1) * C_HEAD_DIM] = (acc_s[h] / l_s[h]).T.astype(o_ref.dtype)


def dsa_prompt(q16, qi16, s32, k16, vt16, kit16, rel_bias, *, n_seq, tq=256):
    m = q16.shape[0]
    seq = m // n_seq
    nq = seq // tq
    nbits = int(math.log2(seq))
    assert 2 ** nbits == seq and seq % tq == 0
    assert tq == SRCH_W and tq >= TOPK
    kern = functools.partial(_dsa_prompt_kernel, tq=tq, nbits=nbits, sk=min(SRCH_KEYS, seq))
    once = pl.Buffered(1)
    return pl.pallas_call(
        kern,
        grid=(n_seq, nq),
        in_specs=[pl.BlockSpec((tq, C_WIDTH), lambda b, i: (b * nq + i, 0)),
                  pl.BlockSpec((tq, IDX_HEADS * LANES), lambda b, i: (b * nq + i, 0)),
                  pl.BlockSpec((tq, LANES), lambda b, i: (b * nq + i, 0)),
                  pl.BlockSpec((1, seq, C_WIDTH), lambda b, i: (b, 0, 0), pipeline_mode=once),
                  pl.BlockSpec((1, C_WIDTH, seq), lambda b, i: (b, 0, 0), pipeline_mode=once),
                  pl.BlockSpec((1, LANES, seq), lambda b, i: (b, 0, 0), pipeline_mode=once),
                  pl.BlockSpec(memory_space=pltpu.SMEM)],
        out_specs=pl.BlockSpec((tq, C_WIDTH), lambda b, i: (b * nq + i, 0)),
        out_shape=jax.ShapeDtypeStruct((m, C_WIDTH), BF16),
        scratch_shapes=[pltpu.VMEM((tq, seq), I16),
                        pltpu.VMEM((tq, seq), I16),
                        pltpu.VMEM((IDX_HEADS, tq, LANES), F32),
                        pltpu.VMEM((2, C_HEADS, tq, tq), F32),
                        pltpu.VMEM((C_HEADS, 1, tq), F32),
                        pltpu.VMEM((C_HEADS, 1, tq), F32),
                        pltpu.VMEM((C_HEADS, C_HEAD_DIM, tq), F32),
                        pltpu.VMEM((tq, SRCH_W), I16),
                        pltpu.VMEM((tq, tq), F32),
                        pltpu.VMEM((tq, SRCH_W), I16),
                        pltpu.VMEM((tq, SRCH_W), I16),
                        pltpu.VMEM((tq, SRCH_W), I16),
                        pltpu.VMEM((tq, SRCH_W), BF16)],
        compiler_params=_cparams(("arbitrary", "arbitrary")),
        name="dsa_prompt",
    )(q16, qi16, s32, k16, vt16, kit16, rel_bias)


NEW_ROWS = 16


def _dsa_sample_kernel(pt_ref, *refs, n_pages, t_new):
    ki_refs = refs[0:n_pages]
    k_refs = refs[n_pages:2 * n_pages]
    v_refs = refs[2 * n_pages:3 * n_pages]
    (qi_ref, wrep_ref, q_ref, kin_ref, knew_ref, vnew_ref, rb_ref, o_ref,
     kiall_s, xk_s, xv_s, lg_s, tab_s, exp_s) = refs[3 * n_pages:]
    b = pl.program_id(0)
    past = n_pages * PAGE
    nkeys = past + PAGE
    n_tiles = n_pages + 1
    nth = t_new * C_HEADS
    pw = PAGE * C_HEADS
    rowx = lax.broadcasted_iota(I32, (nth, pw), 0)
    lanex = lax.broadcasted_iota(I32, (nth, pw), 1)
    own_head = (lanex % C_HEADS) == (rowx % C_HEADS)

    @pl.when(b == 0)
    def _():
        kiall_s[past:nkeys, :] = jnp.zeros((PAGE, LANES), BF16)
        xk_s[past * C_HEADS:nkeys * C_HEADS, :] = jnp.zeros((pw, C_HEAD_DIM), BF16)
        xv_s[past * C_HEADS:nkeys * C_HEADS, :] = jnp.zeros((pw, C_HEAD_DIM), BF16)
        ei = lax.broadcasted_iota(I32, (PAGE, pw), 0)
        ej = lax.broadcasted_iota(I32, (PAGE, pw), 1)
        exp_s[...] = jnp.where(ej // C_HEADS == ei, 1.0, 0.0).astype(BF16)
        t_of = rowx // C_HEADS
        k_of = lanex // C_HEADS
        tab0 = jnp.zeros((nth, pw), F32)
        tab1 = jnp.zeros((nth, pw), F32)
        for h in range(C_HEADS):
            sel = (lanex % C_HEADS) == h
            tab0 = jnp.where(sel, _bias_table(PAGE + t_of - k_of, rb_ref, h), tab0)
            tab1 = jnp.where(sel, _bias_table(t_of - k_of, rb_ref, h), tab1)
        tab_s[0] = tab0 * LOG2E
        tab_s[1] = tab1 * LOG2E

    zpad = jnp.zeros((PAGE, LANES - IDX_DIM), F32)
    for p in range(n_pages):
        kiall_s[p * PAGE:(p + 1) * PAGE, :] = jnp.concatenate([ki_refs[p][0, 0], zpad], axis=1).astype(BF16)
        xk_s[p * pw:(p + 1) * pw, :] = k_refs[p][0, 0].reshape(pw, C_HEAD_DIM).astype(BF16)
        xv_s[p * pw:(p + 1) * pw, :] = v_refs[p][0, 0].reshape(pw, C_HEAD_DIM).astype(BF16)
    kiall_s[past:past + NEW_ROWS, :] = jnp.concatenate(
        [kin_ref[0], jnp.zeros((NEW_ROWS, LANES - IDX_DIM), F32)], axis=1).astype(BF16)
    xk_s[past * C_HEADS:(past + NEW_ROWS) * C_HEADS, :] = knew_ref[0]
    xv_s[past * C_HEADS:(past + NEW_ROWS) * C_HEADS, :] = vnew_ref[0]

    d = _dot_nt(qi_ref[0], kiall_s[...])
    r = jnp.maximum(d, 0.0) * _wide(wrep_ref[0], nkeys)
    sc = jnp.sum(r.reshape(t_new, C_HEADS, nkeys), axis=1)
    rowr = lax.broadcasted_iota(I32, (t_new, nkeys), 0)
    colr = lax.broadcasted_iota(I32, (t_new, nkeys), 1)
    valid = (colr < past) | (colr - past <= rowr)
    kr = _order_key(jnp.where(valid, sc, -jnp.inf))

    def pick(ok1, c1, ok2, c2, ok3, c3, cur):
        return jnp.where(ok3, c3, jnp.where(ok2, c2, jnp.where(ok1, c1, cur)))

    def bit_step(bi, thr):
        one = lax.shift_left(jnp.int32(1), 30 - 2 * bi)
        c1, c2, c3 = thr + one, thr + 2 * one, thr + 3 * one
        cnt = lambda c: jnp.sum(jnp.where(kr >= c, 1, 0), axis=1, keepdims=True)
        return pick(cnt(c1) >= TOPK, c1, cnt(c2) >= TOPK, c2, cnt(c3) >= TOPK, c3, thr)

    thr = lax.fori_loop(0, 16, bit_step, jnp.full((t_new, 1), INT_MIN, I32))
    need = TOPK - jnp.sum(jnp.where(kr > thr, 1, 0), axis=1, keepdims=True)
    tie = kr == thr
    nsteps = (int(math.ceil(math.log2(nkeys))) + 1) // 2

    def idx_step(bi, jv):
        one = lax.shift_left(jnp.int32(1), 2 * (nsteps - 1 - bi))
        c1, c2, c3 = jv + one, jv + 2 * one, jv + 3 * one
        cnt = lambda c: jnp.sum(jnp.where(tie & (colr < c), 1, 0), axis=1, keepdims=True)
        return pick(cnt(c1) < need, c1, cnt(c2) < need, c2, cnt(c3) < need, c3, jv)

    jv = lax.fori_loop(0, nsteps, idx_step, jnp.zeros((t_new, 1), I32))
    sel = ((kr > thr) | (tie & (colr <= jv))) & valid
    sel = jnp.broadcast_to(jnp.where(sel, 1.0, 0.0)[:, None, :], (t_new, C_HEADS, nkeys))
    sel16 = sel.reshape(nth, nkeys).astype(BF16)

    qs = q_ref[0]
    sel_rows = jnp.concatenate([sel16[:, p * PAGE:(p + 1) * PAGE] for p in range(n_tiles)], axis=0)
    selx = _dot(sel_rows, exp_s[...])
    for p in range(n_tiles):
        s = _dot_nt(qs, xk_s[p * pw:(p + 1) * pw, :])
        if p >= n_pages - 1:
            s = s + tab_s[p - (n_pages - 1)]
        lg_s[p] = jnp.where((selx[p * nth:(p + 1) * nth, :] > 0.5) & own_head, s, NEG_BIG)
    mx = jnp.full((nth, 1), NEG_BIG, F32)
    for p in range(n_tiles):
        mx = jnp.maximum(mx, jnp.max(lg_s[p], axis=1, keepdims=True))
    den = jnp.zeros((nth, 1), F32)
    out = jnp.zeros((nth, C_HEAD_DIM), F32)
    for p in range(n_tiles):
        pr = jnp.exp2(lg_s[p] - mx)
        den = den + jnp.sum(pr, axis=1, keepdims=True)
        out = out + _dot(pr.astype(BF16), xv_s[p * pw:(p + 1) * pw, :])
    o_ref[0] = out / den


def sample_dsa_operands(q16, qi16, s32, k16, v16, t_new, db):
    nth = t_new * C_HEADS
    by_seq = lambda a: a.reshape((t_new, db) + a.shape[1:]).swapaxes(0, 1)
    qi = by_seq(qi16.reshape(t_new * db, IDX_HEADS, LANES)).reshape(db, nth, LANES)
    q = by_seq(q16.reshape(t_new * db, C_HEADS, C_HEAD_DIM)).reshape(db, nth, C_HEAD_DIM)
    wscale = IDX_DIM ** -0.5 * IDX_HEADS ** -0.5
    w = by_seq(s32[:, IDX_DIM:IDX_DIM + IDX_HEADS]).reshape(db, nth, 1) * wscale
    wrep = jnp.broadcast_to(w, (db, nth, LANES))
    row_pad = ((0, 0), (0, NEW_ROWS - t_new), (0, 0))
    kin = jnp.pad(by_seq(s32[:, :IDX_DIM]), row_pad)
    new_rows = lambda a: jnp.pad(by_seq(a), row_pad).reshape(db, NEW_ROWS * C_HEADS, C_HEAD_DIM)
    return qi, wrep, q, kin, new_rows(k16), new_rows(v16)


def dsa_sample(page_table, layer, pool_ki, pool_k, pool_v, qi16, wrep, q16, kin, knew16, vnew16, rel_bias, *, t_new):
    db, n_pages = page_table.shape
    n_tiles = n_pages + 1
    nkeys = n_tiles * PAGE
    nth = t_new * C_HEADS
    pw = PAGE * C_HEADS
    kern = functools.partial(_dsa_sample_kernel, n_pages=n_pages, t_new=t_new)

    def ki_spec(p):
        return pl.BlockSpec((1, 1, PAGE, IDX_DIM), lambda b, pt: (layer, pt[b, p], 0, 0))

    def kv_spec(p):
        return pl.BlockSpec((1, 1, PAGE, C_HEADS, C_HEAD_DIM), lambda b, pt: (layer, pt[b, p], 0, 0, 0))

    per_seq = lambda shape: pl.BlockSpec((1,) + shape, lambda b, pt: (b, 0, 0))
    in_specs = ([ki_spec(p) for p in range(n_pages)]
                + [kv_spec(p) for p in range(n_pages)]
                + [kv_spec(p) for p in range(n_pages)]
                + [per_seq((nth, LANES)), per_seq((nth, LANES)), per_seq((nth, C_HEAD_DIM)),
                   per_seq((NEW_ROWS, IDX_DIM)), per_seq((NEW_ROWS * C_HEADS, C_HEAD_DIM)),
                   per_seq((NEW_ROWS * C_HEADS, C_HEAD_DIM)),
                   pl.BlockSpec(memory_space=pltpu.SMEM)])
    grid_spec = pltpu.PrefetchScalarGridSpec(
        num_scalar_prefetch=1,
        grid=(db,),
        in_specs=in_specs,
        out_specs=pl.BlockSpec((1, nth, C_HEAD_DIM), lambda b, pt: (b, 0, 0)),
        scratch_shapes=[pltpu.VMEM((nkeys, LANES), BF16),
                        pltpu.VMEM((nkeys * C_HEADS, C_HEAD_DIM), BF16),
                        pltpu.VMEM((nkeys * C_HEADS, C_HEAD_DIM), BF16),
                        pltpu.VMEM((n_tiles, nth, pw), F32),
                        pltpu.VMEM((2, nth, pw), F32),
                        pltpu.VMEM((PAGE, pw), BF16)])
    return pl.pallas_call(
        kern,
        grid_spec=grid_spec,
        out_shape=jax.ShapeDtypeStruct((db, nth, C_HEAD_DIM), F32),
        compiler_params=_cparams(("arbitrary",)),
        name="dsa_sample",
    )(page_table, *([pool_ki] * n_pages), *([pool_k] * n_pages), *([pool_v] * n_pages),
      qi16, wrep, q16, kin, knew16, vnew16, rel_bias)


PROMPT_TM = 512
FFN_TM = 512
ODD_COLS = 3 * C_WIDTH + IDX_HEADS * LANES + LANES


def _odd_weight(w):
    d = w.shape[0]
    qkv = w[:, :3 * C_WIDTH]
    qi = w[:, 3 * C_WIDTH:3 * C_WIDTH + IDX_HEADS * IDX_DIM].reshape(d, IDX_HEADS, IDX_DIM)
    qi = jnp.pad(qi, ((0, 0), (0, 0), (0, LANES - IDX_DIM))).reshape(d, IDX_HEADS * LANES)
    tail = w[:, 3 * C_WIDTH + IDX_HEADS * IDX_DIM:]
    tail = jnp.pad(tail, ((0, 0), (0, LANES - tail.shape[1])))
    return jnp.concatenate([qkv, qi, tail], axis=1).astype(BF16)


ODD_SEGS = ((0, C_WIDTH, C_HEAD_DIM ** -0.5 * LOG2E, (0,)),
            (C_WIDTH, C_WIDTH, 1.0, (1, 2)),
            (2 * C_WIDTH, C_WIDTH, 1.0, (3, 4)),
            (3 * C_WIDTH, IDX_HEADS * LANES, 1.0, (5,)),
            (3 * C_WIDTH + IDX_HEADS * LANES, LANES, 1.0, (6,)))
ODD_OUTS = ((C_WIDTH, BF16), (C_WIDTH, F32), (C_WIDTH, BF16), (C_WIDTH, F32), (C_WIDTH, BF16),
            (IDX_HEADS * LANES, BF16), (LANES, F32))
EVEN_COLS = 2 * A_WIDTH + 4 * B_WIDTH


def kernel(x_prompt, x_sample, state_conv_a, state_hgrn, cache_k, cache_v, cache_ki, state_ffn_conv, page_table, norm_mix, norm_ffn, norm_final, w_in_even, w_in_odd, w_out, conv_a_w, conv_a_b, ln_a_g, ln_a_b, hgrn_lb, hgrn_gn, rel_bias, w_up, ffn_conv_w, ffn_conv_b, w_down):
    bsz, seq, d = x_prompt.shape
    db, t_new, _ = x_sample.shape
    depth = norm_mix.shape[0]
    n_phys = cache_k.shape[1]
    to_tm = lambda a: jnp.transpose(a, (1, 0, 2))

    xp = x_prompt.reshape(bsz * seq, d)
    xs = to_tm(x_sample).reshape(t_new * db, d)
    lbs = jnp.cumsum(jax.nn.softmax(hgrn_lb.astype(F32), axis=0), axis=0)
    lbs = lbs - lbs[0:1]
    log_lb = jnp.log(lbs)
    log_1m_lb = jnp.log1p(-lbs)
    tiles_per_seq = seq // FFN_TM

    p_conv, p_hg, p_k, p_v, p_ki, p_ffn = [], [], [], [], [], []
    s_conv, s_hg, s_k, s_v, s_ki, s_ffn = [], [], [], [], [], []
    for l in range(depth):
        j = l // 2
        wo16 = w_out[l].astype(BF16)
        if l % 2 == 0:
            w16 = w_in_even[j].astype(BF16)
            segs = ((0, EVEN_COLS, 1.0, (0,)),)
            (zp,) = norm_matmul(xp, norm_mix[l], w16, segs, ((EVEN_COLS, F32),), tm=PROMPT_TM)
            (zs,) = norm_matmul(xs, norm_mix[l], w16, segs, ((EVEN_COLS, F32),), tm=PROMPT_TM)
            conv = (conv_a_w[j], conv_a_b[j], ln_a_g[j], ln_a_b[j])
            ap, cp = mixer_a(zp, *conv, jnp.zeros((bsz, A_CONV - 1, A_WIDTH), F32), n_seq=bsz, dil=1, tm=PROMPT_TM)
            bp, sp = hgrn(zp, 2, log_lb[j], log_1m_lb[j], hgrn_gn[j],
                          jnp.zeros((bsz, B_HEADS, B_DK, B_DK), F32),
                          n_seq=bsz, rt=HG_CHUNK, n_real=HG_CHUNK, nsb=HG_CHUNK // HG_SUB)
            pre_t = to_tm(state_conv_a[j]).reshape(1, (A_CONV - 1) * db, A_WIDTH)
            a_s, cs = mixer_a(zs, *conv, pre_t, n_seq=1, dil=db, tm=t_new * db)
            zb = to_tm(zs.reshape(t_new, db, EVEN_COLS))[:, :, 2 * A_WIDTH:]
            zb = jnp.pad(zb, ((0, 0), (0, 8 - t_new), (0, 0))).reshape(db * 8, 4 * B_WIDTH)
            bs8, ss = hgrn(zb, 0, log_lb[j], log_1m_lb[j], hgrn_gn[j], state_hgrn[j],
                           n_seq=db, rt=8, n_real=t_new, nsb=1)
            b_s = to_tm(bs8.reshape(db, 8, B_WIDTH)[:, :t_new]).reshape(t_new * db, B_WIDTH)
            xp = outproj(xp, [ap, bp], [wo16[:A_WIDTH], wo16[A_WIDTH:]], tm=PROMPT_TM)
            xs = outproj(xs, [a_s, b_s], [wo16[:A_WIDTH], wo16[A_WIDTH:]], tm=PROMPT_TM)
            p_conv.append(cp)
            p_hg.append(sp)
            s_conv.append(to_tm(cs.reshape(A_CONV - 1, db, A_WIDTH)))
            s_hg.append(ss)
        else:
            w16 = _odd_weight(w_in_odd[j])
            q16, k32, k16, v32, v16, qi16, s32 = norm_matmul(xp, norm_mix[l], w16, ODD_SEGS, ODD_OUTS, tm=PROMPT_TM)
            kit16 = jnp.pad(jnp.transpose(s32[:, :IDX_DIM].reshape(bsz, seq, IDX_DIM), (0, 2, 1)),
                            ((0, 0), (0, LANES - IDX_DIM), (0, 0))).astype(BF16)
            vt16 = jnp.transpose(v16.reshape(bsz, seq, C_WIDTH), (0, 2, 1))
            mp = dsa_prompt(q16, qi16, s32, k16.reshape(bsz, seq, C_WIDTH), vt16, kit16, rel_bias, n_seq=bsz)
            p_k.append(k32.reshape(bsz, seq, C_HEADS, C_HEAD_DIM))
            p_v.append(v32.reshape(bsz, seq, C_HEADS, C_HEAD_DIM))
            p_ki.append(s32[:, :IDX_DIM].reshape(bsz, seq, IDX_DIM))
            q16s, k32s, k16s, v32s, v16s, qi16s, s32s = norm_matmul(xs, norm_mix[l], w16, ODD_SEGS, ODD_OUTS,
                                                                   tm=PROMPT_TM)
            ops = sample_dsa_operands(q16s, qi16s, s32s, k16s, v16s, t_new, db)
            o_s = dsa_sample(page_table, j, cache_ki, cache_k, cache_v, *ops, rel_bias, t_new=t_new)
            m_s = to_tm(o_s.reshape(db, t_new, C_WIDTH)).reshape(t_new * db, C_WIDTH)
            xp = outproj(xp, [mp], [wo16], tm=PROMPT_TM)
            xs = outproj(xs, [m_s], [wo16], tm=PROMPT_TM)
            s_k.append(to_tm(k32s.reshape(t_new, db, C_WIDTH)).reshape(db, t_new, C_HEADS, C_HEAD_DIM))
            s_v.append(to_tm(v32s.reshape(t_new, db, C_WIDTH)).reshape(db, t_new, C_HEADS, C_HEAD_DIM))
            s_ki.append(to_tm(s32s[:, :IDX_DIM].reshape(t_new, db, IDX_DIM)))
        ffn_w = (w_up[l].astype(BF16), ffn_conv_w[l], ffn_conv_b[l], w_down[l].astype(BF16))
        xp, fcp = ffn(xp, norm_ffn[l], *ffn_w, jnp.zeros((bsz, 2, 2 * D_FF), F32), n_seq=bsz, dil=1, tm=FFN_TM)
        pre_t = to_tm(state_ffn_conv[l]).reshape(1, 2 * db, 2 * D_FF)
        xs, fcs = ffn(xs, norm_ffn[l], *ffn_w, pre_t, n_seq=1, dil=db, tm=t_new * db)
        p_ffn.append(fcp[tiles_per_seq - 1::tiles_per_seq])
        s_ffn.append(to_tm(fcs.reshape(2, db, 2 * D_FF)))
    y_prompt = rmsnorm_call(xp, norm_final, tm=PROMPT_TM).reshape(bsz, seq, d)
    y_sample = to_tm(rmsnorm_call(xs, norm_final, tm=PROMPT_TM).reshape(t_new, db, d))
    return (y_prompt, y_sample,
            jnp.stack(p_conv), jnp.stack(p_hg), jnp.stack(p_k), jnp.stack(p_v), jnp.stack(p_ki), jnp.stack(p_ffn),
            jnp.stack(s_conv), jnp.stack(s_hg), jnp.stack(s_k), jnp.stack(s_v), jnp.stack(s_ki), jnp.stack(s_ffn))
```

```python
import functools
import math

import jax
import jax.numpy as jnp
from jax import lax
from jax.experimental import pallas as pl
from jax.experimental.pallas import tpu as pltpu

F32 = jnp.float32
BF16 = jnp.bfloat16
I32 = jnp.int32

EPS = 1e-6
D_MODEL = 1024
A_WIDTH = 512
A_CONV = 31
B_HEADS = 4
B_DK = 128
B_WIDTH = 512
C_HEADS = 8
C_HEAD_DIM = 128
C_WIDTH = 1024
IDX_HEADS = 8
IDX_DIM = 64
TOPK = 256
REL_BUCKETS = 32
REL_MAX_DIST = 128
D_FF = 2816
PAGE = 128

LANES = 128
VMEM_LIMIT = 60 * 1024 * 1024
NEG_BIG = -1e30
INT_MIN = -2 ** 31


def _cparams(sem):
    return pltpu.CompilerParams(dimension_semantics=sem, vmem_limit_bytes=VMEM_LIMIT)


def _rms(x, g):
    return x * lax.rsqrt(jnp.mean(x * x, axis=-1, keepdims=True) + EPS) * g


def _sigmoid(x):
    return 1.0 / (1.0 + jnp.exp(-x))


def _dot(a, b):
    return jnp.dot(a, b, preferred_element_type=F32)


def _dot_nt(a, b):
    return lax.dot_general(a, b, (((1,), (1,)), ((), ())), preferred_element_type=F32)


def _norm_matmul_kernel(x_ref, g_ref, w_ref, *out_refs, segs):
    hb = _rms(x_ref[...], g_ref[...]).astype(BF16)
    for start, width, scale, outs in segs:
        for c0 in range(0, width, 512):
            cw = min(512, width - c0)
            r = _dot(hb, w_ref[:, start + c0:start + c0 + cw])
            if scale != 1.0:
                r = r * scale
            for oi in outs:
                out_refs[oi][:, c0:c0 + cw] = r.astype(out_refs[oi].dtype)


def norm_matmul(x, g, w16, segs, out_defs, tm=512):
    m, d = x.shape
    tm = min(tm, m)
    n = w16.shape[1]
    return pl.pallas_call(
        functools.partial(_norm_matmul_kernel, segs=segs),
        grid=(m // tm,),
        in_specs=[pl.BlockSpec((tm, d), lambda i: (i, 0)),
                  pl.BlockSpec((1, d), lambda i: (0, 0)),
                  pl.BlockSpec((d, n), lambda i: (0, 0))],
        out_specs=[pl.BlockSpec((tm, wd), lambda i: (i, 0)) for wd, _ in out_defs],
        out_shape=[jax.ShapeDtypeStruct((m, wd), dt) for wd, dt in out_defs],
        compiler_params=_cparams(("arbitrary",)),
        name="norm_matmul",
    )(x, g.reshape(1, d), w16)


def _outproj_kernel(*refs, n_in):
    x_ref = refs[0]
    o_ref = refs[-1]
    acc = x_ref[...]
    for i in range(n_in):
        acc = acc + _dot(refs[1 + 2 * i][...].astype(BF16), refs[2 + 2 * i][...])
    o_ref[...] = acc


def outproj(x, ms, ws, tm=512):
    m, d = x.shape
    tm = min(tm, m)
    in_specs = [pl.BlockSpec((tm, d), lambda i: (i, 0))]
    args = [x]
    for mm, w in zip(ms, ws):
        in_specs.append(pl.BlockSpec((tm, mm.shape[1]), lambda i: (i, 0)))
        in_specs.append(pl.BlockSpec(w.shape, lambda i: (0, 0)))
        args += [mm, w]
    return pl.pallas_call(
        functools.partial(_outproj_kernel, n_in=len(ms)),
        grid=(m // tm,),
        in_specs=in_specs,
        out_specs=pl.BlockSpec((tm, d), lambda i: (i, 0)),
        out_shape=jax.ShapeDtypeStruct((m, d), F32),
        compiler_params=_cparams(("arbitrary",)),
        name="outproj",
    )(*args)


FFN_SUB = 256


def _ffn_kernel(x_ref, g_ref, wup_ref, cw_ref, cb_ref, wdn_ref, pre_ref, o_ref, npre_ref,
                ubuf_s, *, tm, dil, pad, tiles_per_seq):
    first = (pl.program_id(0) % tiles_per_seq) == 0

    @pl.when(first)
    def _():
        ubuf_s[pad - 2 * dil:pad, :] = pre_ref[0]

    sub = min(tm, FFN_SUB)
    for r in range(tm // sub):
        r0 = pad + r * sub
        xr = x_ref[r * sub:(r + 1) * sub, :]
        u = _dot(_rms(xr, g_ref[...]).astype(BF16), wup_ref[...])
        ubuf_s[r0:r0 + sub, :] = u
        uc = (u * cw_ref[2:3, :] + ubuf_s[r0 - dil:r0 - dil + sub, :] * cw_ref[1:2, :]
              + ubuf_s[r0 - 2 * dil:r0 - 2 * dil + sub, :] * cw_ref[0:1, :] + cb_ref[...])
        gate = uc[:, :D_FF]
        val = uc[:, D_FF:]
        act = (gate * _sigmoid(gate) * val).astype(BF16)
        o_ref[r * sub:(r + 1) * sub, :] = xr + _dot(act, wdn_ref[...])
    tail = ubuf_s[pad + tm - 2 * dil:pad + tm, :]
    npre_ref[0] = tail
    if tiles_per_seq > 1:
        ubuf_s[pad - 2 * dil:pad, :] = tail


def ffn(x, g, wup16, cw, cb, wdn16, pre, *, n_seq, dil, tm):
    m, d = x.shape
    f2 = 2 * D_FF
    tiles_per_seq = (m // n_seq) // tm
    pad = max(8, 2 * dil)
    kern = functools.partial(_ffn_kernel, tm=tm, dil=dil, pad=pad, tiles_per_seq=tiles_per_seq)
    once = pl.Buffered(1)
    return pl.pallas_call(
        kern,
        grid=(m // tm,),
        in_specs=[pl.BlockSpec((tm, d), lambda i: (i, 0)),
                  pl.BlockSpec((1, d), lambda i: (0, 0)),
                  pl.BlockSpec((d, f2), lambda i: (0, 0), pipeline_mode=once),
                  pl.BlockSpec((3, f2), lambda i: (0, 0)),
                  pl.BlockSpec((1, f2), lambda i: (0, 0)),
                  pl.BlockSpec((D_FF, d), lambda i: (0, 0), pipeline_mode=once),
                  pl.BlockSpec((1, 2 * dil, f2), lambda i: (i // tiles_per_seq, 0, 0))],
        out_specs=[pl.BlockSpec((tm, d), lambda i: (i, 0)),
                   pl.BlockSpec((1, 2 * dil, f2), lambda i: (i, 0, 0))],
        out_shape=[jax.ShapeDtypeStruct((m, d), F32),
                   jax.ShapeDtypeStruct((m // tm, 2 * dil, f2), F32)],
        scratch_shapes=[pltpu.VMEM((pad + tm, f2), F32)],
        compiler_params=_cparams(("arbitrary",)),
        name="conv_ffn",
    )(x, g.reshape(1, d), wup16, cw, cb.reshape(1, f2), wdn16, pre)


def _rmsnorm_kernel(x_ref, g_ref, o_ref):
    o_ref[...] = _rms(x_ref[...], g_ref[...])


def rmsnorm_call(x, g, tm=512):
    m, d = x.shape
    tm = min(tm, m)
    return pl.pallas_call(
        _rmsnorm_kernel,
        grid=(m // tm,),
        in_specs=[pl.BlockSpec((tm, d), lambda i: (i, 0)), pl.BlockSpec((1, d), lambda i: (0, 0))],
        out_specs=pl.BlockSpec((tm, d), lambda i: (i, 0)),
        out_shape=jax.ShapeDtypeStruct((m, d), F32),
        compiler_params=_cparams(("arbitrary",)),
        name="final_rmsnorm",
    )(x, g.reshape(1, d))


def _mixer_a_kernel(val_ref, gate_ref, cw_ref, cb_ref, lg_ref, lb_ref, pre_ref, o_ref, npre_ref,
                    gbuf_s, *, tm, dil, pad, tiles_per_seq):
    i = pl.program_id(0)
    first = (i % tiles_per_seq) == 0
    hist = (A_CONV - 1) * dil

    @pl.when(first)
    def _():
        gbuf_s[pad - hist:pad, :] = pre_ref[0]

    glu = val_ref[...] * _sigmoid(gate_ref[...])
    gbuf_s[pad:pad + tm, :] = glu
    npre_ref[0] = gbuf_s[pad + tm - hist:pad + tm, :]
    acc = jnp.zeros((tm, A_WIDTH), F32) + cb_ref[...]
    for j in range(A_CONV):
        s0 = pad - hist + j * dil
        acc = acc + gbuf_s[s0:s0 + tm, :] * cw_ref[j:j + 1, :]
    mu = jnp.mean(acc, axis=-1, keepdims=True)
    xc = acc - mu
    var = jnp.mean(xc * xc, axis=-1, keepdims=True)
    y = xc * lax.rsqrt(var + EPS) * lg_ref[...] + lb_ref[...]
    o_ref[...] = (y * _sigmoid(y)).astype(o_ref.dtype)
    if tiles_per_seq > 1:
        gbuf_s[0:pad, :] = gbuf_s[tm:tm + pad, :]


def mixer_a(z, cw, cb, lg, lb, pre, *, n_seq, dil, tm):
    m = z.shape[0]
    tiles_per_seq = (m // n_seq) // tm
    hist = (A_CONV - 1) * dil
    pad = 32 if dil == 1 else hist
    kern = functools.partial(_mixer_a_kernel, tm=tm, dil=dil, pad=pad, tiles_per_seq=tiles_per_seq)
    vec = lambda a: a.reshape(1, A_WIDTH)
    return pl.pallas_call(
        kern,
        grid=(m // tm,),
        in_specs=[pl.BlockSpec((tm, A_WIDTH), lambda i: (i, 0)),
                  pl.BlockSpec((tm, A_WIDTH), lambda i: (i, 1)),
                  pl.BlockSpec((A_CONV, A_WIDTH), lambda i: (0, 0)),
                  pl.BlockSpec((1, A_WIDTH), lambda i: (0, 0)),
                  pl.BlockSpec((1, A_WIDTH), lambda i: (0, 0)),
                  pl.BlockSpec((1, A_WIDTH), lambda i: (0, 0)),
                  pl.BlockSpec((1, hist, A_WIDTH), lambda i: (i // tiles_per_seq, 0, 0))],
        out_specs=[pl.BlockSpec((tm, A_WIDTH), lambda i: (i, 0)),
                   pl.BlockSpec((1, hist, A_WIDTH), lambda i: (i // tiles_per_seq, 0, 0))],
        out_shape=[jax.ShapeDtypeStruct((m, A_WIDTH), BF16),
                   jax.ShapeDtypeStruct((n_seq, hist, A_WIDTH), F32)],
        scratch_shapes=[pltpu.VMEM((pad + tm, A_WIDTH), F32)],
        compiler_params=_cparams(("arbitrary",)),
        name="mixer_a",
    )(z, z, cw, vec(cb), vec(lg), vec(lb), pre)


HG_CHUNK = 128
HG_SUB = 16


def _split3(x):
    hi = x.astype(BF16)
    r1 = x - hi.astype(F32)
    mid = r1.astype(BF16)
    lo = (r1 - mid.astype(F32)).astype(BF16)
    return hi, mid, lo


def _hgrn_chunk(qr, fr, ir, gr, llb, l1m, gn, st_s, b_s, q_s, k_s, v_s, *, n_real, nsb):
    c = HG_CHUNK
    q = qr * _sigmoid(qr)
    ls = jnp.minimum(fr, 0.0) - jnp.log1p(jnp.exp(-jnp.abs(fr)))
    bt = l1m + ls
    logf = jnp.maximum(llb, bt) + jnp.log1p(jnp.exp(-jnp.abs(llb - bt)))
    kk = 1.0 - jnp.exp(logf)
    if n_real < c:
        real = lax.broadcasted_iota(I32, (c, B_WIDTH), 0) < n_real
        logf = jnp.where(real, logf, 0.0)
        kk = jnp.where(real, kk, 0.0)
    ri = lax.broadcasted_iota(I32, (c, c), 0)
    ci = lax.broadcasted_iota(I32, (c, c), 1)
    tril = jnp.where(ci <= ri, 1.0, 0.0).astype(BF16)
    hi, mid, lo = _split3(logf)
    b = _dot(tril, hi) + _dot(tril, mid) + _dot(tril, lo)
    b_s[...] = b
    q_s[...] = q
    k_s[...] = kk
    v_s[...] = ir
    sub = HG_SUB
    row16 = lax.broadcasted_iota(I32, (sub, 1), 0)
    outs = []
    for h in range(B_HEADS):
        sl = slice(h * B_DK, (h + 1) * B_DK)
        st = st_s[h]
        o_carry = _dot_nt((q[:, sl] * jnp.exp(b[:, sl])).astype(BF16), st.astype(BF16))
        o_rows = []
        for i in range(nsb):
            r0 = i * sub
            qb = q_s[r0:r0 + sub, sl]
            bb = b_s[r0:r0 + sub, sl]
            o_blk = jnp.zeros((sub, B_DK), F32)
            if i > 0:
                ref_b = b_s[r0 - 1:r0, sl]
                qt = (qb * jnp.exp(bb - ref_b)).astype(BF16)
                kt = (k_s[0:r0, sl] * jnp.exp(ref_b - b_s[0:r0, sl])).astype(BF16)
                att = _dot_nt(qt, kt)
                o_blk = o_blk + _dot(att.astype(BF16), v_s[0:r0, sl].astype(BF16))
            for s in range(sub):
                r = r0 + s
                if r >= n_real:
                    continue
                p = qb * jnp.exp(bb - b_s[r:r + 1, sl]) * k_s[r:r + 1, sl]
                a = jnp.sum(p, axis=1, keepdims=True)
                a = jnp.where(row16 >= s, a, 0.0)
                o_blk = o_blk + a * v_s[r:r + 1, sl]
            o_rows.append(o_carry[r0:r0 + sub, :] + o_blk)
        if nsb * sub < c:
            o_rows.append(o_carry[nsb * sub:, :])
        o = jnp.concatenate(o_rows, axis=0)
        on = o * lax.rsqrt(jnp.mean(o * o, axis=-1, keepdims=True) + EPS) * gn
        g = gr[:, sl]
        outs.append(on * (g * _sigmoid(g)))
        bl = b_s[c - 1:c, sl]
        kd = (k_s[:, sl] * jnp.exp(bl - b_s[:, sl])).astype(BF16)
        vt = v_s[:, sl].T.astype(BF16)
        st_s[h] = st * jnp.exp(bl) + _dot(vt, kd)
    return jnp.concatenate(outs, axis=1)


def _hgrn_kernel(q_ref, f_ref, i_ref, g_ref, llb_ref, l1m_ref, gn_ref, s0_ref, o_ref, sout_ref,
                 st_s, b_s, q_s, k_s, v_s, *, rt, n_real, nsb, tiles_per_seq):
    c = HG_CHUNK
    first = (pl.program_id(0) % tiles_per_seq) == 0

    @pl.when(first)
    def _():
        for h in range(B_HEADS):
            st_s[h] = s0_ref[0, h].T

    rows = min(rt, c)
    for ch in range(max(1, rt // c)):
        rs = slice(ch * c, ch * c + rows)

        def load(ref):
            x = ref[rs, :]
            if rows < c:
                x = jnp.concatenate([x, jnp.zeros((c - rows, B_WIDTH), F32)], axis=0)
            return x

        out = _hgrn_chunk(load(q_ref), load(f_ref), load(i_ref), load(g_ref),
                          llb_ref[...], l1m_ref[...], gn_ref[...],
                          st_s, b_s, q_s, k_s, v_s, n_real=n_real, nsb=nsb)
        o_ref[rs, :] = out[0:rows, :].astype(o_ref.dtype)
    for h in range(B_HEADS):
        sout_ref[0, h] = st_s[h].T


def hgrn(z, col0, llb, l1m, gn, s0, *, n_seq, rt, n_real, nsb):
    m = z.shape[0]
    tiles_per_seq = (m // n_seq) // rt
    kern = functools.partial(_hgrn_kernel, rt=rt, n_real=n_real, nsb=nsb, tiles_per_seq=tiles_per_seq)
    c = HG_CHUNK
    zspec = lambda k: pl.BlockSpec((rt, B_WIDTH), lambda i: (i, col0 + k))
    state_spec = pl.BlockSpec((1, B_HEADS, B_DK, B_DK), lambda i: (i // tiles_per_seq, 0, 0, 0))
    return pl.pallas_call(
        kern,
        grid=(m // rt,),
        in_specs=[zspec(0), zspec(1), zspec(2), zspec(3),
                  pl.BlockSpec((1, B_WIDTH), lambda i: (0, 0)),
                  pl.BlockSpec((1, B_WIDTH), lambda i: (0, 0)),
                  pl.BlockSpec((1, B_DK), lambda i: (0, 0)),
                  state_spec],
        out_specs=[pl.BlockSpec((rt, B_WIDTH), lambda i: (i, 0)), state_spec],
        out_shape=[jax.ShapeDtypeStruct((m, B_WIDTH), BF16),
                   jax.ShapeDtypeStruct((n_seq, B_HEADS, B_DK, B_DK), F32)],
        scratch_shapes=[pltpu.VMEM((B_HEADS, B_DK, B_DK), F32),
                        pltpu.VMEM((c, B_WIDTH), F32),
                        pltpu.VMEM((c, B_WIDTH), F32),
                        pltpu.VMEM((c, B_WIDTH), F32),
                        pltpu.VMEM((c, B_WIDTH), F32)],
        compiler_params=_cparams(("arbitrary",)),
        name="hgrn2",
    )(z, z, z, z, llb.reshape(1, B_WIDTH), l1m.reshape(1, B_WIDTH), gn.reshape(1, B_DK), s0)


def _order_key(x):
    b = lax.bitcast_convert_type(x + 0.0, I32)
    return jnp.where(b < 0, b ^ 0x7FFFFFFF, b)


def _t5_bucket(dist):
    n = jnp.maximum(dist, 0)
    max_exact = REL_BUCKETS // 2
    nf = jnp.maximum(n, 1).astype(F32)
    large = max_exact + (jnp.log(nf / max_exact) / math.log(REL_MAX_DIST / max_exact)
                         * (REL_BUCKETS - max_exact)).astype(I32)
    large = jnp.minimum(large, REL_BUCKETS - 1)
    return jnp.where(n < max_exact, n, large)


def _bias_table(dist, rb_ref, h):
    bucket = _t5_bucket(dist)
    last = rb_ref[REL_BUCKETS - 1, h]
    out = jnp.zeros(dist.shape, F32)
    for bk in range(REL_BUCKETS - 1):
        out = jnp.where(bucket == bk, rb_ref[bk, h] - last, out)
    return out


LOG2E = math.log2(math.e)
ATT_COLS = 128
I16 = jnp.int16
I16_MIN = -2 ** 15
SRCH_ROWS = 64
SRCH_W = 256
SRCH_KEYS = 2048


def _wide(x, n):
    return jnp.concatenate([x] * (n // LANES), axis=1)


def _dsa_prompt_kernel(q_ref, qi_ref, s_ref, k_ref, vt_ref, kit_ref, rb_ref, o_ref,
                       hi_s, lo_s, wb_s, bias_s, m_s, l_s, acc_s, j_s, mb_s, cand_s, thi_s, tlo_s, cnt_s,
                       *, tq, nbits, sk):
    b = pl.program_id(0)
    i = pl.program_id(1)
    row = lax.broadcasted_iota(I32, (tq, tq), 0)
    col = lax.broadcasted_iota(I32, (tq, tq), 1)

    @pl.when((b == 0) & (i == 0))
    def _():
        for h in range(C_HEADS):
            bias_s[0, h] = _bias_table(col + tq - row, rb_ref, h) * LOG2E
            bias_s[1, h] = _bias_table(col - row, rb_ref, h) * LOG2E

    @pl.when(i == 0)
    def _():
        hi_s[...] = jnp.full(hi_s.shape, I16_MIN, I16)
        lo_s[...] = jnp.full(lo_s.shape, I16_MIN, I16)

    wscale = IDX_DIM ** -0.5 * IDX_HEADS ** -0.5
    for h in range(IDX_HEADS):
        wcol = s_ref[:, IDX_DIM + h:IDX_DIM + h + 1] * wscale
        wb_s[h] = jnp.broadcast_to(wcol, (tq, LANES))

    def score_tile(j, carry):
        off = pl.multiple_of(j * tq, tq)
        kt = kit_ref[0, :, pl.ds(off, tq)]
        sc = jnp.zeros((tq, tq), F32)
        for h in range(IDX_HEADS):
            d = _dot(qi_ref[:, h * LANES:(h + 1) * LANES], kt)
            sc = sc + jnp.maximum(d, 0.0) * _wide(wb_s[h], tq)
        sc = jnp.where((j == i) & (col > row), -jnp.inf, sc)
        key = _order_key(sc)
        hi_s[:, pl.ds(off, tq)] = lax.shift_right_arithmetic(key, 16).astype(I16)
        lo_s[:, pl.ds(off, tq)] = ((key & 0xFFFF) + I16_MIN).astype(I16)
        return carry

    lax.fori_loop(0, i + 1, score_tile, 0)

    n_steps = ((i + 1) * tq + sk - 1) // sk
    ones = jnp.ones((SRCH_W, LANES), BF16)

    def scan(fn, init=None):
        for rc in range(tq // SRCH_ROWS):
            rows = slice(rc * SRCH_ROWS, (rc + 1) * SRCH_ROWS)

            def body(j, acc):
                for q in range(sk // SRCH_W):
                    off = pl.multiple_of(j * sk, sk) + q * SRCH_W
                    acc = fn(rows, off, acc)
                return acc

            out = lax.fori_loop(0, n_steps, body, init)
            if init is not None:
                cnt_s[rows, :] = out

    def count(buf, pred):
        def fn(rows, off, acc):
            return acc + jnp.where(pred(buf[rows, pl.ds(off, SRCH_W)], cand_s[rows, :]), BF16(1), BF16(0))
        scan(fn, jnp.zeros((SRCH_ROWS, SRCH_W), BF16))
        return _dot(cnt_s[...], ones)

    def search(buf, base, n_init):
        def bit_step(bi, carry):
            thr, n_ge = carry
            cand = thr + lax.shift_left(jnp.int32(1), 15 - bi)
            cand_s[...] = _wide(cand.astype(I16), SRCH_W)
            cnt = count(buf, lambda kq, c: kq >= c) + base
            ok = cnt >= TOPK
            return jnp.where(ok, cand, thr), jnp.where(ok, cnt, n_ge)
        return lax.fori_loop(0, 16, bit_step, (jnp.full((tq, LANES), I16_MIN, I32), n_init))

    n_all = jnp.zeros((tq, LANES), F32) + ((i + 1) * tq).astype(F32)
    thi, n_ge_hi = search(hi_s, 0.0, n_all)
    thi_s[...] = _wide(thi.astype(I16), SRCH_W)
    cand_s[...] = thi_s[...]
    n_gt_hi = count(hi_s, lambda kq, c: kq > c)

    def keep_tied_hi(rows, off, acc):
        tied = hi_s[rows, pl.ds(off, SRCH_W)] == thi_s[rows, :]
        lo_s[rows, pl.ds(off, SRCH_W)] = jnp.where(tied, lo_s[rows, pl.ds(off, SRCH_W)], I16(I16_MIN))
        return acc

    scan(keep_tied_hi)
    tlo, n_ge = search(lo_s, n_gt_hi, n_ge_hi)
    tlo_s[...] = _wide(tlo.astype(I16), SRCH_W)
    cand_s[...] = tlo_s[...]
    n_gt = count(lo_s, lambda kq, c: kq > c) + n_gt_hi

    need = TOPK - n_gt
    j_s[...] = jnp.full((tq, SRCH_W), 2 ** 14, I16)
    any_tie = jnp.max(jnp.where(n_ge > TOPK, 1.0, 0.0)) > 0.0

    @pl.when(any_tie)
    def _():
        lane16 = lax.broadcasted_iota(I32, (SRCH_ROWS, SRCH_W), 1)

        def tied_below(rows, off, acc):
            tied = ((hi_s[rows, pl.ds(off, SRCH_W)] == thi_s[rows, :])
                    & (lo_s[rows, pl.ds(off, SRCH_W)] == tlo_s[rows, :])
                    & ((lane16 + off).astype(I16) < cand_s[rows, :]))
            return acc + jnp.where(tied, BF16(1), BF16(0))

        def idx_step(bi, jv):
            cand = jv + lax.shift_left(jnp.int32(1), nbits - 1 - bi)
            cand_s[...] = _wide(cand.astype(I16), SRCH_W)
            scan(tied_below, jnp.zeros((SRCH_ROWS, SRCH_W), BF16))
            cnt = _dot(cnt_s[...], ones)
            return jnp.where(cnt < need, cand, jv)

        jv = lax.fori_loop(0, nbits, idx_step, jnp.zeros((tq, LANES), I32))
        j_s[...] = _wide(jv.astype(I16), SRCH_W)

    m_s[...] = jnp.full(m_s.shape, NEG_BIG, F32)
    l_s[...] = jnp.zeros(l_s.shape, F32)
    acc_s[...] = jnp.zeros(acc_s.shape, F32)
    rowg16 = (row + i * tq).astype(I16)

    def att_tile(j, tab):
        off = pl.multiple_of(j * tq, tq)
        kh16 = hi_s[:, pl.ds(off, tq)]
        kl16 = lo_s[:, pl.ds(off, tq)]
        colg16 = (col + off).astype(I16)
        thi16, tlo16 = thi_s[...], tlo_s[...]
        mask = ((kh16 > thi16) | ((kh16 == thi16) & ((kl16 > tlo16) | ((kl16 == tlo16) & (colg16 <= j_s[...])))))
        mask = mask & (colg16 <= rowg16)
        mb_s[...] = (jnp.where(mask, I16(0), I16(-1)).astype(F32) * (-NEG_BIG)).T
        for h in range(C_HEADS):
            hs = slice(h * C_HEAD_DIM, (h + 1) * C_HEAD_DIM)
            kh = k_ref[0, pl.ds(off, tq), hs]
            vth = vt_ref[0, hs, pl.ds(off, tq)]
            for qc in range(tq // ATT_COLS):
                qs = slice(qc * ATT_COLS, (qc + 1) * ATT_COLS)
                s = _dot_nt(kh, q_ref[qs, hs]) + mb_s[:, qs]
                if tab is not None:
                    s = s + bias_s[tab, h, :, qs]
                m_prev = m_s[h, :, qs]
                m_new = jnp.maximum(m_prev, jnp.max(s, axis=0, keepdims=True))
                alpha = jnp.exp2(m_prev - m_new)
                p = jnp.exp2(s - m_new)
                l_s[h, :, qs] = alpha * l_s[h, :, qs] + jnp.sum(p, axis=0, keepdims=True)
                acc_s[h, :, qs] = alpha * acc_s[h, :, qs] + _dot(vth, p.astype(BF16))
                m_s[h, :, qs] = m_new

    def far_tile(j, carry):
        att_tile(j, None)
        return carry

    lax.fori_loop(0, jnp.maximum(i - 1, 0), far_tile, 0)

    @pl.when(i >= 1)
    def _():
        att_tile(i - 1, 0)

    att_tile(i, 1)
    for h in range(C_HEADS):
        o_ref[:, h * C_HEAD_DIM:(h + 1) * C_HEAD_DIM] = (acc_s[h] / l_s[h]).T.astype(o_ref.dtype)


def dsa_prompt(q16, qi16, s32, k16, vt16, kit16, rel_bias, *, n_seq, tq=256):
    m = q16.shape[0]
    seq = m // n_seq
    nq = seq // tq
    nbits = int(math.log2(seq))
    assert 2 ** nbits == seq and seq % tq == 0
    assert tq == SRCH_W and tq >= TOPK
    kern = functools.partial(_dsa_prompt_kernel, tq=tq, nbits=nbits, sk=min(SRCH_KEYS, seq))
    once = pl.Buffered(1)
    return pl.pallas_call(
        kern,
        grid=(n_seq, nq),
        in_specs=[pl.BlockSpec((tq, C_WIDTH), lambda b, i: (b * nq + i, 0)),
                  pl.BlockSpec((tq, IDX_HEADS * LANES), lambda b, i: (b * nq + i, 0)),
                  pl.BlockSpec((tq, LANES), lambda b, i: (b * nq + i, 0)),
                  pl.BlockSpec((1, seq, C_WIDTH), lambda b, i: (b, 0, 0), pipeline_mode=once),
                  pl.BlockSpec((1, C_WIDTH, seq), lambda b, i: (b, 0, 0), pipeline_mode=once),
                  pl.BlockSpec((1, LANES, seq), lambda b, i: (b, 0, 0), pipeline_mode=once),
                  pl.BlockSpec(memory_space=pltpu.SMEM)],
        out_specs=pl.BlockSpec((tq, C_WIDTH), lambda b, i: (b * nq + i, 0)),
        out_shape=jax.ShapeDtypeStruct((m, C_WIDTH), BF16),
        scratch_shapes=[pltpu.VMEM((tq, seq), I16),
                        pltpu.VMEM((tq, seq), I16),
                        pltpu.VMEM((IDX_HEADS, tq, LANES), F32),
                        pltpu.VMEM((2, C_HEADS, tq, tq), F32),
                        pltpu.VMEM((C_HEADS, 1, tq), F32),
                        pltpu.VMEM((C_HEADS, 1, tq), F32),
                        pltpu.VMEM((C_HEADS, C_HEAD_DIM, tq), F32),
                        pltpu.VMEM((tq, SRCH_W), I16),
                        pltpu.VMEM((tq, tq), F32),
                        pltpu.VMEM((tq, SRCH_W), I16),
                        pltpu.VMEM((tq, SRCH_W), I16),
                        pltpu.VMEM((tq, SRCH_W), I16),
                        pltpu.VMEM((tq, SRCH_W), BF16)],
        compiler_params=_cparams(("arbitrary", "arbitrary")),
        name="dsa_prompt",
    )(q16, qi16, s32, k16, vt16, kit16, rel_bias)


NEW_ROWS = 16


def _dsa_sample_kernel(pt_ref, *refs, n_pages, t_new):
    ki_refs = refs[0:n_pages]
    k_refs = refs[n_pages:2 * n_pages]
    v_refs = refs[2 * n_pages:3 * n_pages]
    (qi_ref, wrep_ref, q_ref, kin_ref, knew_ref, vnew_ref, rb_ref, o_ref,
     kiall_s, xk_s, xv_s, lg_s, tab_s, exp_s) = refs[3 * n_pages:]
    b = pl.program_id(0)
    past = n_pages * PAGE
    nkeys = past + PAGE
    n_tiles = n_pages + 1
    nth = t_new * C_HEADS
    pw = PAGE * C_HEADS
    rowx = lax.broadcasted_iota(I32, (nth, pw), 0)
    lanex = lax.broadcasted_iota(I32, (nth, pw), 1)
    own_head = (lanex % C_HEADS) == (rowx % C_HEADS)

    @pl.when(b == 0)
    def _():
        kiall_s[past:nkeys, :] = jnp.zeros((PAGE, LANES), BF16)
        xk_s[past * C_HEADS:nkeys * C_HEADS, :] = jnp.zeros((pw, C_HEAD_DIM), BF16)
        xv_s[past * C_HEADS:nkeys * C_HEADS, :] = jnp.zeros((pw, C_HEAD_DIM), BF16)
        ei = lax.broadcasted_iota(I32, (PAGE, pw), 0)
        ej = lax.broadcasted_iota(I32, (PAGE, pw), 1)
        exp_s[...] = jnp.where(ej // C_HEADS == ei, 1.0, 0.0).astype(BF16)
        t_of = rowx // C_HEADS
        k_of = lanex // C_HEADS
        tab0 = jnp.zeros((nth, pw), F32)
        tab1 = jnp.zeros((nth, pw), F32)
        for h in range(C_HEADS):
            sel = (lanex % C_HEADS) == h
            tab0 = jnp.where(sel, _bias_table(PAGE + t_of - k_of, rb_ref, h), tab0)
            tab1 = jnp.where(sel, _bias_table(t_of - k_of, rb_ref, h), tab1)
        tab_s[0] = tab0 * LOG2E
        tab_s[1] = tab1 * LOG2E

    zpad = jnp.zeros((PAGE, LANES - IDX_DIM), F32)
    for p in range(n_pages):
        kiall_s[p * PAGE:(p + 1) * PAGE, :] = jnp.concatenate([ki_refs[p][0, 0], zpad], axis=1).astype(BF16)
        xk_s[p * pw:(p + 1) * pw, :] = k_refs[p][0, 0].reshape(pw, C_HEAD_DIM).astype(BF16)
        xv_s[p * pw:(p + 1) * pw, :] = v_refs[p][0, 0].reshape(pw, C_HEAD_DIM).astype(BF16)
    kiall_s[past:past + NEW_ROWS, :] = jnp.concatenate(
        [kin_ref[0], jnp.zeros((NEW_ROWS, LANES - IDX_DIM), F32)], axis=1).astype(BF16)
    xk_s[past * C_HEADS:(past + NEW_ROWS) * C_HEADS, :] = knew_ref[0]
    xv_s[past * C_HEADS:(past + NEW_ROWS) * C_HEADS, :] = vnew_ref[0]

    d = _dot_nt(qi_ref[0], kiall_s[...])
    r = jnp.maximum(d, 0.0) * _wide(wrep_ref[0], nkeys)
    sc = jnp.sum(r.reshape(t_new, C_HEADS, nkeys), axis=1)
    rowr = lax.broadcasted_iota(I32, (t_new, nkeys), 0)
    colr = lax.broadcasted_iota(I32, (t_new, nkeys), 1)
    valid = (colr < past) | (colr - past <= rowr)
    kr = _order_key(jnp.where(valid, sc, -jnp.inf))

    def pick(ok1, c1, ok2, c2, ok3, c3, cur):
        return jnp.where(ok3, c3, jnp.where(ok2, c2, jnp.where(ok1, c1, cur)))

    def wrap32(v):
        return ((v + 2 ** 31) % 2 ** 32) - 2 ** 31

    thr = jnp.full((t_new, 1), INT_MIN, I32)
    for bi in range(16):
        one = 1 << (30 - 2 * bi)
        c1, c2, c3 = thr + one, thr + wrap32(2 * one), thr + wrap32(3 * one)
        cnt = lambda c: jnp.sum(jnp.where(kr >= c, 1, 0), axis=1, keepdims=True)
        thr = pick(cnt(c1) >= TOPK, c1, cnt(c2) >= TOPK, c2, cnt(c3) >= TOPK, c3, thr)
    need = TOPK - jnp.sum(jnp.where(kr > thr, 1, 0), axis=1, keepdims=True)
    tie = kr == thr
    nsteps = (int(math.ceil(math.log2(nkeys))) + 1) // 2
    jv = jnp.zeros((t_new, 1), I32)
    for bi in range(nsteps):
        one = 1 << (2 * (nsteps - 1 - bi))
        c1, c2, c3 = jv + one, jv + 2 * one, jv + 3 * one
        cnt = lambda c: jnp.sum(jnp.where(tie & (colr < c), 1, 0), axis=1, keepdims=True)
        jv = pick(cnt(c1) < need, c1, cnt(c2) < need, c2, cnt(c3) < need, c3, jv)
    sel = ((kr > thr) | (tie & (colr <= jv))) & valid
    sel = jnp.broadcast_to(jnp.where(sel, 1.0, 0.0)[:, None, :], (t_new, C_HEADS, nkeys))
    sel16 = sel.reshape(nth, nkeys).astype(BF16)

    qs = q_ref[0]
    sel_rows = jnp.concatenate([sel16[:, p * PAGE:(p + 1) * PAGE] for p in range(n_tiles)], axis=0)
    selx = _dot(sel_rows, exp_s[...])
    for p in range(n_tiles):
        s = _dot_nt(qs, xk_s[p * pw:(p + 1) * pw, :])
        if p >= n_pages - 1:
            s = s + tab_s[p - (n_pages - 1)]
        lg_s[p] = jnp.where((selx[p * nth:(p + 1) * nth, :] > 0.5) & own_head, s, NEG_BIG)
    mx = jnp.full((nth, 1), NEG_BIG, F32)
    for p in range(n_tiles):
        mx = jnp.maximum(mx, jnp.max(lg_s[p], axis=1, keepdims=True))
    den = jnp.zeros((nth, 1), F32)
    out = jnp.zeros((nth, C_HEAD_DIM), F32)
    for p in range(n_tiles):
        pr = jnp.exp2(lg_s[p] - mx)
        den = den + jnp.sum(pr, axis=1, keepdims=True)
        out = out + _dot(pr.astype(BF16), xv_s[p * pw:(p + 1) * pw, :])
    o_ref[0] = out / den


def sample_dsa_operands(q16, qi16, s32, k16, v16, t_new, db):
    nth = t_new * C_HEADS
    by_seq = lambda a: a.reshape((t_new, db) + a.shape[1:]).swapaxes(0, 1)
    qi = by_seq(qi16.reshape(t_new * db, IDX_HEADS, LANES)).reshape(db, nth, LANES)
    q = by_seq(q16.reshape(t_new * db, C_HEADS, C_HEAD_DIM)).reshape(db, nth, C_HEAD_DIM)
    wscale = IDX_DIM ** -0.5 * IDX_HEADS ** -0.5
    w = by_seq(s32[:, IDX_DIM:IDX_DIM + IDX_HEADS]).reshape(db, nth, 1) * wscale
    wrep = jnp.broadcast_to(w, (db, nth, LANES))
    row_pad = ((0, 0), (0, NEW_ROWS - t_new), (0, 0))
    kin = jnp.pad(by_seq(s32[:, :IDX_DIM]), row_pad)
    new_rows = lambda a: jnp.pad(by_seq(a), row_pad).reshape(db, NEW_ROWS * C_HEADS, C_HEAD_DIM)
    return qi, wrep, q, kin, new_rows(k16), new_rows(v16)


def dsa_sample(page_table, layer, pool_ki, pool_k, pool_v, qi16, wrep, q16, kin, knew16, vnew16, rel_bias, *, t_new):
    db, n_pages = page_table.shape
    n_tiles = n_pages + 1
    nkeys = n_tiles * PAGE
    nth = t_new * C_HEADS
    pw = PAGE * C_HEADS
    kern = functools.partial(_dsa_sample_kernel, n_pages=n_pages, t_new=t_new)

    def ki_spec(p):
        return pl.BlockSpec((1, 1, PAGE, IDX_DIM), lambda b, pt: (layer, pt[b, p], 0, 0))

    def kv_spec(p):
        return pl.BlockSpec((1, 1, PAGE, C_HEADS, C_HEAD_DIM), lambda b, pt: (layer, pt[b, p], 0, 0, 0))

    per_seq = lambda shape: pl.BlockSpec((1,) + shape, lambda b, pt: (b, 0, 0))
    in_specs = ([ki_spec(p) for p in range(n_pages)]
                + [kv_spec(p) for p in range(n_pages)]
                + [kv_spec(p) for p in range(n_pages)]
                + [per_seq((nth, LANES)), per_seq((nth, LANES)), per_seq((nth, C_HEAD_DIM)),
                   per_seq((NEW_ROWS, IDX_DIM)), per_seq((NEW_ROWS * C_HEADS, C_HEAD_DIM)),
                   per_seq((NEW_ROWS * C_HEADS, C_HEAD_DIM)),
                   pl.BlockSpec(memory_space=pltpu.SMEM)])
    grid_spec = pltpu.PrefetchScalarGridSpec(
        num_scalar_prefetch=1,
        grid=(db,),
        in_specs=in_specs,
        out_specs=pl.BlockSpec((1, nth, C_HEAD_DIM), lambda b, pt: (b, 0, 0)),
        scratch_shapes=[pltpu.VMEM((nkeys, LANES), BF16),
                        pltpu.VMEM((nkeys * C_HEADS, C_HEAD_DIM), BF16),
                        pltpu.VMEM((nkeys * C_HEADS, C_HEAD_DIM), BF16),
                        pltpu.VMEM((n_tiles, nth, pw), F32),
                        pltpu.VMEM((2, nth, pw), F32),
                        pltpu.VMEM((PAGE, pw), BF16)])
    return pl.pallas_call(
        kern,
        grid_spec=grid_spec,
        out_shape=jax.ShapeDtypeStruct((db, nth, C_HEAD_DIM), F32),
        compiler_params=_cparams(("arbitrary",)),
        name="dsa_sample",
    )(page_table, *([pool_ki] * n_pages), *([pool_k] * n_pages), *([pool_v] * n_pages),
      qi16, wrep, q16, kin, knew16, vnew16, rel_bias)


PROMPT_TM = 512
FFN_TM = 512
ODD_COLS = 3 * C_WIDTH + IDX_HEADS * LANES + LANES


def _odd_weight(w):
    d = w.shape[0]
    qkv = w[:, :3 * C_WIDTH]
    qi = w[:, 3 * C_WIDTH:3 * C_WIDTH + IDX_HEADS * IDX_DIM].reshape(d, IDX_HEADS, IDX_DIM)
    qi = jnp.pad(qi, ((0, 0), (0, 0), (0, LANES - IDX_DIM))).reshape(d, IDX_HEADS * LANES)
    tail = w[:, 3 * C_WIDTH + IDX_HEADS * IDX_DIM:]
    tail = jnp.pad(tail, ((0, 0), (0, LANES - tail.shape[1])))
    return jnp.concatenate([qkv, qi, tail], axis=1).astype(BF16)


ODD_SEGS = ((0, C_WIDTH, C_HEAD_DIM ** -0.5 * LOG2E, (0,)),
            (C_WIDTH, C_WIDTH, 1.0, (1, 2)),
            (2 * C_WIDTH, C_WIDTH, 1.0, (3, 4)),
            (3 * C_WIDTH, IDX_HEADS * LANES, 1.0, (5,)),
            (3 * C_WIDTH + IDX_HEADS * LANES, LANES, 1.0, (6,)))
ODD_OUTS = ((C_WIDTH, BF16), (C_WIDTH, F32), (C_WIDTH, BF16), (C_WIDTH, F32), (C_WIDTH, BF16),
            (IDX_HEADS * LANES, BF16), (LANES, F32))
EVEN_COLS = 2 * A_WIDTH + 4 * B_WIDTH


def kernel(x_prompt, x_sample, state_conv_a, state_hgrn, cache_k, cache_v, cache_ki, state_ffn_conv, page_table, norm_mix, norm_ffn, norm_final, w_in_even, w_in_odd, w_out, conv_a_w, conv_a_b, ln_a_g, ln_a_b, hgrn_lb, hgrn_gn, rel_bias, w_up, ffn_conv_w, ffn_conv_b, w_down):
    bsz, seq, d = x_prompt.shape
    db, t_new, _ = x_sample.shape
    depth = norm_mix.shape[0]
    n_phys = cache_k.shape[1]
    to_tm = lambda a: jnp.transpose(a, (1, 0, 2))

    xp = x_prompt.reshape(bsz * seq, d)
    xs = to_tm(x_sample).reshape(t_new * db, d)
    lbs = jnp.cumsum(jax.nn.softmax(hgrn_lb.astype(F32), axis=0), axis=0)
    lbs = lbs - lbs[0:1]
    log_lb = jnp.log(lbs)
    log_1m_lb = jnp.log1p(-lbs)
    tiles_per_seq = seq // FFN_TM

    p_conv, p_hg, p_k, p_v, p_ki, p_ffn = [], [], [], [], [], []
    s_conv, s_hg, s_k, s_v, s_ki, s_ffn = [], [], [], [], [], []
    for l in range(depth):
        j = l // 2
        wo16 = w_out[l].astype(BF16)
        if l % 2 == 0:
            w16 = w_in_even[j].astype(BF16)
            segs = ((0, EVEN_COLS, 1.0, (0,)),)
            (zp,) = norm_matmul(xp, norm_mix[l], w16, segs, ((EVEN_COLS, F32),), tm=PROMPT_TM)
            (zs,) = norm_matmul(xs, norm_mix[l], w16, segs, ((EVEN_COLS, F32),), tm=PROMPT_TM)
            conv = (conv_a_w[j], conv_a_b[j], ln_a_g[j], ln_a_b[j])
            ap, cp = mixer_a(zp, *conv, jnp.zeros((bsz, A_CONV - 1, A_WIDTH), F32), n_seq=bsz, dil=1, tm=PROMPT_TM)
            bp, sp = hgrn(zp, 2, log_lb[j], log_1m_lb[j], hgrn_gn[j],
                          jnp.zeros((bsz, B_HEADS, B_DK, B_DK), F32),
                          n_seq=bsz, rt=HG_CHUNK, n_real=HG_CHUNK, nsb=HG_CHUNK // HG_SUB)
            pre_t = to_tm(state_conv_a[j]).reshape(1, (A_CONV - 1) * db, A_WIDTH)
            a_s, cs = mixer_a(zs, *conv, pre_t, n_seq=1, dil=db, tm=t_new * db)
            zb = to_tm(zs.reshape(t_new, db, EVEN_COLS))[:, :, 2 * A_WIDTH:]
            zb = jnp.pad(zb, ((0, 0), (0, 8 - t_new), (0, 0))).reshape(db * 8, 4 * B_WIDTH)
            bs8, ss = hgrn(zb, 0, log_lb[j], log_1m_lb[j], hgrn_gn[j], state_hgrn[j],
                           n_seq=db, rt=8, n_real=t_new, nsb=1)
            b_s = to_tm(bs8.reshape(db, 8, B_WIDTH)[:, :t_new]).reshape(t_new * db, B_WIDTH)
            xp = outproj(xp, [ap, bp], [wo16[:A_WIDTH], wo16[A_WIDTH:]], tm=PROMPT_TM)
            xs = outproj(xs, [a_s, b_s], [wo16[:A_WIDTH], wo16[A_WIDTH:]], tm=PROMPT_TM)
            p_conv.append(cp)
            p_hg.append(sp)
            s_conv.append(to_tm(cs.reshape(A_CONV - 1, db, A_WIDTH)))
            s_hg.append(ss)
        else:
            w16 = _odd_weight(w_in_odd[j])
            q16, k32, k16, v32, v16, qi16, s32 = norm_matmul(xp, norm_mix[l], w16, ODD_SEGS, ODD_OUTS, tm=PROMPT_TM)
            kit16 = jnp.pad(jnp.transpose(s32[:, :IDX_DIM].reshape(bsz, seq, IDX_DIM), (0, 2, 1)),
                            ((0, 0), (0, LANES - IDX_DIM), (0, 0))).astype(BF16)
            vt16 = jnp.transpose(v16.reshape(bsz, seq, C_WIDTH), (0, 2, 1))
            mp = dsa_prompt(q16, qi16, s32, k16.reshape(bsz, seq, C_WIDTH), vt16, kit16, rel_bias, n_seq=bsz)
            p_k.append(k32.reshape(bsz, seq, C_HEADS, C_HEAD_DIM))
            p_v.append(v32.reshape(bsz, seq, C_HEADS, C_HEAD_DIM))
            p_ki.append(s32[:, :IDX_DIM].reshape(bsz, seq, IDX_DIM))
            q16s, k32s, k16s, v32s, v16s, qi16s, s32s = norm_matmul(xs, norm_mix[l], w16, ODD_SEGS, ODD_OUTS,
                                                                   tm=PROMPT_TM)
            ops = sample_dsa_operands(q16s, qi16s, s32s, k16s, v16s, t_new, db)
            o_s = dsa_sample(page_table, j, cache_ki, cache_k, cache_v, *ops, rel_bias, t_new=t_new)
            m_s = to_tm(o_s.reshape(db, t_new, C_WIDTH)).reshape(t_new * db, C_WIDTH)
            xp = outproj(xp, [mp], [wo16], tm=PROMPT_TM)
            xs = outproj(xs, [m_s], [wo16], tm=PROMPT_TM)
            s_k.append(to_tm(k32s.reshape(t_new, db, C_WIDTH)).reshape(db, t_new, C_HEADS, C_HEAD_DIM))
            s_v.append(to_tm(v32s.reshape(t_new, db, C_WIDTH)).reshape(db, t_new, C_HEADS, C_HEAD_DIM))
            s_ki.append(to_tm(s32s[:, :IDX_DIM].reshape(t_new, db, IDX_DIM)))
        ffn_w = (w_up[l].astype(BF16), ffn_conv_w[l], ffn_conv_b[l], w_down[l].astype(BF16))
        xp, fcp = ffn(xp, norm_ffn[l], *ffn_w, jnp.zeros((bsz, 2, 2 * D_FF), F32), n_seq=bsz, dil=1, tm=FFN_TM)
        pre_t = to_tm(state_ffn_conv[l]).reshape(1, 2 * db, 2 * D_FF)
        xs, fcs = ffn(xs, norm_ffn[l], *ffn_w, pre_t, n_seq=1, dil=db, tm=t_new * db)
        p_ffn.append(fcp[tiles_per_seq - 1::tiles_per_seq])
        s_ffn.append(to_tm(fcs.reshape(2, db, 2 * D_FF)))
    y_prompt = rmsnorm_call(xp, norm_final, tm=PROMPT_TM).reshape(bsz, seq, d)
    y_sample = to_tm(rmsnorm_call(xs, norm_final, tm=PROMPT_TM).reshape(t_new, db, d))
    return (y_prompt, y_sample,
            jnp.stack(p_conv), jnp.stack(p_hg), jnp.stack(p_k), jnp.stack(p_v), jnp.stack(p_ki), jnp.stack(p_ffn),
            jnp.stack(s_conv), jnp.stack(s_hg), jnp.stack(s_k), jnp.stack(s_v), jnp.stack(s_ki), jnp.stack(s_ffn))
```

```python
import functools
import math

import jax
import jax.numpy as jnp
from jax import lax
from jax.experimental import pallas as pl
from jax.experimental.pallas import tpu as pltpu

F32 = jnp.float32
BF16 = jnp.bfloat16
I32 = jnp.int32

EPS = 1e-6
D_MODEL = 1024
A_WIDTH = 512
A_CONV = 31
B_HEADS = 4
B_DK = 128
B_WIDTH = 512
C_HEADS = 8
C_HEAD_DIM = 128
C_WIDTH = 1024
IDX_HEADS = 8
IDX_DIM = 64
TOPK = 256
REL_BUCKETS = 32
REL_MAX_DIST = 128
D_FF = 2816
PAGE = 128

LANES = 128
VMEM_LIMIT = 60 * 1024 * 1024
NEG_BIG = -1e30
INT_MIN = -2 ** 31


def _cparams(sem):
    return pltpu.CompilerParams(dimension_semantics=sem, vmem_limit_bytes=VMEM_LIMIT)


def _rms(x, g):
    return x * lax.rsqrt(jnp.mean(x * x, axis=-1, keepdims=True) + EPS) * g


def _sigmoid(x):
    return 1.0 / (1.0 + jnp.exp(-x))


def _dot(a, b):
    return jnp.dot(a, b, preferred_element_type=F32)


def _dot_nt(a, b):
    return lax.dot_general(a, b, (((1,), (1,)), ((), ())), preferred_element_type=F32)


def _norm_matmul_kernel(x_ref, g_ref, w_ref, *out_refs, segs):
    hb = _rms(x_ref[...], g_ref[...]).astype(BF16)
    for start, width, scale, outs in segs:
        for c0 in range(0, width, 512):
            cw = min(512, width - c0)
            r = _dot(hb, w_ref[:, start + c0:start + c0 + cw])
            if scale != 1.0:
                r = r * scale
            for oi in outs:
                out_refs[oi][:, c0:c0 + cw] = r.astype(out_refs[oi].dtype)


def norm_matmul(x, g, w16, segs, out_defs, tm=512):
    m, d = x.shape
    tm = min(tm, m)
    n = w16.shape[1]
    return pl.pallas_call(
        functools.partial(_norm_matmul_kernel, segs=segs),
        grid=(m // tm,),
        in_specs=[pl.BlockSpec((tm, d), lambda i: (i, 0)),
                  pl.BlockSpec((1, d), lambda i: (0, 0)),
                  pl.BlockSpec((d, n), lambda i: (0, 0))],
        out_specs=[pl.BlockSpec((tm, wd), lambda i: (i, 0)) for wd, _ in out_defs],
        out_shape=[jax.ShapeDtypeStruct((m, wd), dt) for wd, dt in out_defs],
        compiler_params=_cparams(("arbitrary",)),
        name="norm_matmul",
    )(x, g.reshape(1, d), w16)


def _outproj_kernel(*refs, n_in):
    x_ref = refs[0]
    o_ref = refs[-1]
    acc = x_ref[...]
    for i in range(n_in):
        acc = acc + _dot(refs[1 + 2 * i][...].astype(BF16), refs[2 + 2 * i][...])
    o_ref[...] = acc


def outproj(x, ms, ws, tm=512):
    m, d = x.shape
    tm = min(tm, m)
    in_specs = [pl.BlockSpec((tm, d), lambda i: (i, 0))]
    args = [x]
    for mm, w in zip(ms, ws):
        in_specs.append(pl.BlockSpec((tm, mm.shape[1]), lambda i: (i, 0)))
        in_specs.append(pl.BlockSpec(w.shape, lambda i: (0, 0)))
        args += [mm, w]
    return pl.pallas_call(
        functools.partial(_outproj_kernel, n_in=len(ms)),
        grid=(m // tm,),
        in_specs=in_specs,
        out_specs=pl.BlockSpec((tm, d), lambda i: (i, 0)),
        out_shape=jax.ShapeDtypeStruct((m, d), F32),
        compiler_params=_cparams(("arbitrary",)),
        name="outproj",
    )(*args)


FFN_SUB = 256


def _ffn_kernel(x_ref, g_ref, wup_ref, cw_ref, cb_ref, wdn_ref, pre_ref, o_ref, npre_ref,
                ubuf_s, *, tm, dil, pad, tiles_per_seq):
    first = (pl.program_id(0) % tiles_per_seq) == 0

    @pl.when(first)
    def _():
        ubuf_s[pad - 2 * dil:pad, :] = pre_ref[0]

    sub = min(tm, FFN_SUB)
    for r in range(tm // sub):
        r0 = pad + r * sub
        xr = x_ref[r * sub:(r + 1) * sub, :]
        u = _dot(_rms(xr, g_ref[...]).astype(BF16), wup_ref[...])
        ubuf_s[r0:r0 + sub, :] = u
        uc = (u * cw_ref[2:3, :] + ubuf_s[r0 - dil:r0 - dil + sub, :] * cw_ref[1:2, :]
              + ubuf_s[r0 - 2 * dil:r0 - 2 * dil + sub, :] * cw_ref[0:1, :] + cb_ref[...])
        gate = uc[:, :D_FF]
        val = uc[:, D_FF:]
        act = (gate * _sigmoid(gate) * val).astype(BF16)
        o_ref[r * sub:(r + 1) * sub, :] = xr + _dot(act, wdn_ref[...])
    tail = ubuf_s[pad + tm - 2 * dil:pad + tm, :]
    npre_ref[0] = tail
    if tiles_per_seq > 1:
        ubuf_s[pad - 2 * dil:pad, :] = tail


def ffn(x, g, wup16, cw, cb, wdn16, pre, *, n_seq, dil, tm):
    m, d = x.shape
    f2 = 2 * D_FF
    tiles_per_seq = (m // n_seq) // tm
    pad = max(8, 2 * dil)
    kern = functools.partial(_ffn_kernel, tm=tm, dil=dil, pad=pad, tiles_per_seq=tiles_per_seq)
    once = pl.Buffered(1)
    return pl.pallas_call(
        kern,
        grid=(m // tm,),
        in_specs=[pl.BlockSpec((tm, d), lambda i: (i, 0)),
                  pl.BlockSpec((1, d), lambda i: (0, 0)),
                  pl.BlockSpec((d, f2), lambda i: (0, 0), pipeline_mode=once),
                  pl.BlockSpec((3, f2), lambda i: (0, 0)),
                  pl.BlockSpec((1, f2), lambda i: (0, 0)),
                  pl.BlockSpec((D_FF, d), lambda i: (0, 0), pipeline_mode=once),
                  pl.BlockSpec((1, 2 * dil, f2), lambda i: (i // tiles_per_seq, 0, 0))],
        out_specs=[pl.BlockSpec((tm, d), lambda i: (i, 0)),
                   pl.BlockSpec((1, 2 * dil, f2), lambda i: (i, 0, 0))],
        out_shape=[jax.ShapeDtypeStruct((m, d), F32),
                   jax.ShapeDtypeStruct((m // tm, 2 * dil, f2), F32)],
        scratch_shapes=[pltpu.VMEM((pad + tm, f2), F32)],
        compiler_params=_cparams(("arbitrary",)),
        name="conv_ffn",
    )(x, g.reshape(1, d), wup16, cw, cb.reshape(1, f2), wdn16, pre)


def _rmsnorm_kernel(x_ref, g_ref, o_ref):
    o_ref[...] = _rms(x_ref[...], g_ref[...])


def rmsnorm_call(x, g, tm=512):
    m, d = x.shape
    tm = min(tm, m)
    return pl.pallas_call(
        _rmsnorm_kernel,
        grid=(m // tm,),
        in_specs=[pl.BlockSpec((tm, d), lambda i: (i, 0)), pl.BlockSpec((1, d), lambda i: (0, 0))],
        out_specs=pl.BlockSpec((tm, d), lambda i: (i, 0)),
        out_shape=jax.ShapeDtypeStruct((m, d), F32),
        compiler_params=_cparams(("arbitrary",)),
        name="final_rmsnorm",
    )(x, g.reshape(1, d))


def _mixer_a_kernel(val_ref, gate_ref, cw_ref, cb_ref, lg_ref, lb_ref, pre_ref, o_ref, npre_ref,
                    gbuf_s, *, tm, dil, pad, tiles_per_seq):
    i = pl.program_id(0)
    first = (i % tiles_per_seq) == 0
    hist = (A_CONV - 1) * dil

    @pl.when(first)
    def _():
        gbuf_s[pad - hist:pad, :] = pre_ref[0]

    glu = val_ref[...] * _sigmoid(gate_ref[...])
    gbuf_s[pad:pad + tm, :] = glu
    npre_ref[0] = gbuf_s[pad + tm - hist:pad + tm, :]
    acc = jnp.zeros((tm, A_WIDTH), F32) + cb_ref[...]
    for j in range(A_CONV):
        s0 = pad - hist + j * dil
        acc = acc + gbuf_s[s0:s0 + tm, :] * cw_ref[j:j + 1, :]
    mu = jnp.mean(acc, axis=-1, keepdims=True)
    xc = acc - mu
    var = jnp.mean(xc * xc, axis=-1, keepdims=True)
    y = xc * lax.rsqrt(var + EPS) * lg_ref[...] + lb_ref[...]
    o_ref[...] = (y * _sigmoid(y)).astype(o_ref.dtype)
    if tiles_per_seq > 1:
        gbuf_s[0:pad, :] = gbuf_s[tm:tm + pad, :]


def mixer_a(z, cw, cb, lg, lb, pre, *, n_seq, dil, tm):
    m = z.shape[0]
    tiles_per_seq = (m // n_seq) // tm
    hist = (A_CONV - 1) * dil
    pad = 32 if dil == 1 else hist
    kern = functools.partial(_mixer_a_kernel, tm=tm, dil=dil, pad=pad, tiles_per_seq=tiles_per_seq)
    vec = lambda a: a.reshape(1, A_WIDTH)
    return pl.pallas_call(
        kern,
        grid=(m // tm,),
        in_specs=[pl.BlockSpec((tm, A_WIDTH), lambda i: (i, 0)),
                  pl.BlockSpec((tm, A_WIDTH), lambda i: (i, 1)),
                  pl.BlockSpec((A_CONV, A_WIDTH), lambda i: (0, 0)),
                  pl.BlockSpec((1, A_WIDTH), lambda i: (0, 0)),
                  pl.BlockSpec((1, A_WIDTH), lambda i: (0, 0)),
                  pl.BlockSpec((1, A_WIDTH), lambda i: (0, 0)),
                  pl.BlockSpec((1, hist, A_WIDTH), lambda i: (i // tiles_per_seq, 0, 0))],
        out_specs=[pl.BlockSpec((tm, A_WIDTH), lambda i: (i, 0)),
                   pl.BlockSpec((1, hist, A_WIDTH), lambda i: (i // tiles_per_seq, 0, 0))],
        out_shape=[jax.ShapeDtypeStruct((m, A_WIDTH), BF16),
                   jax.ShapeDtypeStruct((n_seq, hist, A_WIDTH), F32)],
        scratch_shapes=[pltpu.VMEM((pad + tm, A_WIDTH), F32)],
        compiler_params=_cparams(("arbitrary",)),
        name="mixer_a",
    )(z, z, cw, vec(cb), vec(lg), vec(lb), pre)


HG_CHUNK = 128
HG_SUB = 16


def _split3(x):
    hi = x.astype(BF16)
    r1 = x - hi.astype(F32)
    mid = r1.astype(BF16)
    lo = (r1 - mid.astype(F32)).astype(BF16)
    return hi, mid, lo


def _hgrn_chunk(qr, fr, ir, gr, llb, l1m, gn, st_s, b_s, q_s, k_s, v_s, *, n_real, nsb):
    c = HG_CHUNK
    q = qr * _sigmoid(qr)
    ls = jnp.minimum(fr, 0.0) - jnp.log1p(jnp.exp(-jnp.abs(fr)))
    bt = l1m + ls
    logf = jnp.maximum(llb, bt) + jnp.log1p(jnp.exp(-jnp.abs(llb - bt)))
    kk = 1.0 - jnp.exp(logf)
    if n_real < c:
        real = lax.broadcasted_iota(I32, (c, B_WIDTH), 0) < n_real
        logf = jnp.where(real, logf, 0.0)
        kk = jnp.where(real, kk, 0.0)
    ri = lax.broadcasted_iota(I32, (c, c), 0)
    ci = lax.broadcasted_iota(I32, (c, c), 1)
    tril = jnp.where(ci <= ri, 1.0, 0.0).astype(BF16)
    hi, mid, lo = _split3(logf)
    b = _dot(tril, hi) + _dot(tril, mid) + _dot(tril, lo)
    b_s[...] = b
    q_s[...] = q
    k_s[...] = kk
    v_s[...] = ir
    sub = HG_SUB
    row16 = lax.broadcasted_iota(I32, (sub, 1), 0)
    outs = []
    for h in range(B_HEADS):
        sl = slice(h * B_DK, (h + 1) * B_DK)
        st = st_s[h]
        o_carry = _dot_nt((q[:, sl] * jnp.exp(b[:, sl])).astype(BF16), st.astype(BF16))
        o_rows = []
        for i in range(nsb):
            r0 = i * sub
            qb = q_s[r0:r0 + sub, sl]
            bb = b_s[r0:r0 + sub, sl]
            o_blk = jnp.zeros((sub, B_DK), F32)
            if i > 0:
                ref_b = b_s[r0 - 1:r0, sl]
                qt = (qb * jnp.exp(bb - ref_b)).astype(BF16)
                kt = (k_s[0:r0, sl] * jnp.exp(ref_b - b_s[0:r0, sl])).astype(BF16)
                att = _dot_nt(qt, kt)
                o_blk = o_blk + _dot(att.astype(BF16), v_s[0:r0, sl].astype(BF16))
            for s in range(sub):
                r = r0 + s
                if r >= n_real:
                    continue
                p = qb * jnp.exp(bb - b_s[r:r + 1, sl]) * k_s[r:r + 1, sl]
                a = jnp.sum(p, axis=1, keepdims=True)
                a = jnp.where(row16 >= s, a, 0.0)
                o_blk = o_blk + a * v_s[r:r + 1, sl]
            o_rows.append(o_carry[r0:r0 + sub, :] + o_blk)
        if nsb * sub < c:
            o_rows.append(o_carry[nsb * sub:, :])
        o = jnp.concatenate(o_rows, axis=0)
        on = o * lax.rsqrt(jnp.mean(o * o, axis=-1, keepdims=True) + EPS) * gn
        g = gr[:, sl]
        outs.append(on * (g * _sigmoid(g)))
        bl = b_s[c - 1:c, sl]
        kd = (k_s[:, sl] * jnp.exp(bl - b_s[:, sl])).astype(BF16)
        vt = v_s[:, sl].T.astype(BF16)
        st_s[h] = st * jnp.exp(bl) + _dot(vt, kd)
    return jnp.concatenate(outs, axis=1)


def _hgrn_kernel(q_ref, f_ref, i_ref, g_ref, llb_ref, l1m_ref, gn_ref, s0_ref, o_ref, sout_ref,
                 st_s, b_s, q_s, k_s, v_s, *, rt, n_real, nsb, tiles_per_seq):
    c = HG_CHUNK
    first = (pl.program_id(0) % tiles_per_seq) == 0

    @pl.when(first)
    def _():
        for h in range(B_HEADS):
            st_s[h] = s0_ref[0, 0, h].T

    rows = min(rt, c)
    for ch in range(max(1, rt // c)):
        rs = slice(ch * c, ch * c + rows)

        def load(ref):
            x = ref[rs, :]
            if rows < c:
                x = jnp.concatenate([x, jnp.zeros((c - rows, B_WIDTH), F32)], axis=0)
            return x

        out = _hgrn_chunk(load(q_ref), load(f_ref), load(i_ref), load(g_ref),
                          llb_ref[...], l1m_ref[...], gn_ref[...],
                          st_s, b_s, q_s, k_s, v_s, n_real=n_real, nsb=nsb)
        o_ref[rs, :] = out[0:rows, :].astype(o_ref.dtype)
    for h in range(B_HEADS):
        sout_ref[0, h] = st_s[h].T


def hgrn(z, col0, llb, l1m, gn, s0, layer, *, n_seq, rt, n_real, nsb):
    m = z.shape[0]
    tiles_per_seq = (m // n_seq) // rt
    kern = functools.partial(_hgrn_kernel, rt=rt, n_real=n_real, nsb=nsb, tiles_per_seq=tiles_per_seq)
    c = HG_CHUNK
    zspec = lambda k: pl.BlockSpec((rt, B_WIDTH), lambda i: (i, col0 + k))
    state_spec = pl.BlockSpec((1, B_HEADS, B_DK, B_DK), lambda i: (i // tiles_per_seq, 0, 0, 0))
    return pl.pallas_call(
        kern,
        grid=(m // rt,),
        in_specs=[zspec(0), zspec(1), zspec(2), zspec(3),
                  pl.BlockSpec((1, B_WIDTH), lambda i: (0, 0)),
                  pl.BlockSpec((1, B_WIDTH), lambda i: (0, 0)),
                  pl.BlockSpec((1, B_DK), lambda i: (0, 0)),
                  pl.BlockSpec((1, 1, B_HEADS, B_DK, B_DK), lambda i: (layer, i // tiles_per_seq, 0, 0, 0))],
        out_specs=[pl.BlockSpec((rt, B_WIDTH), lambda i: (i, 0)), state_spec],
        out_shape=[jax.ShapeDtypeStruct((m, B_WIDTH), BF16),
                   jax.ShapeDtypeStruct((n_seq, B_HEADS, B_DK, B_DK), F32)],
        scratch_shapes=[pltpu.VMEM((B_HEADS, B_DK, B_DK), F32),
                        pltpu.VMEM((c, B_WIDTH), F32),
                        pltpu.VMEM((c, B_WIDTH), F32),
                        pltpu.VMEM((c, B_WIDTH), F32),
                        pltpu.VMEM((c, B_WIDTH), F32)],
        compiler_params=_cparams(("arbitrary",)),
        name="hgrn2",
    )(z, z, z, z, llb.reshape(1, B_WIDTH), l1m.reshape(1, B_WIDTH), gn.reshape(1, B_DK), s0)


def _order_key(x):
    b = lax.bitcast_convert_type(x + 0.0, I32)
    return jnp.where(b < 0, b ^ 0x7FFFFFFF, b)


def _t5_bucket(dist):
    n = jnp.maximum(dist, 0)
    max_exact = REL_BUCKETS // 2
    nf = jnp.maximum(n, 1).astype(F32)
    large = max_exact + (jnp.log(nf / max_exact) / math.log(REL_MAX_DIST / max_exact)
                         * (REL_BUCKETS - max_exact)).astype(I32)
    large = jnp.minimum(large, REL_BUCKETS - 1)
    return jnp.where(n < max_exact, n, large)


def _bias_table(dist, rb_ref, h):
    bucket = _t5_bucket(dist)
    last = rb_ref[REL_BUCKETS - 1, h]
    out = jnp.zeros(dist.shape, F32)
    for bk in range(REL_BUCKETS - 1):
        out = jnp.where(bucket == bk, rb_ref[bk, h] - last, out)
    return out


LOG2E = math.log2(math.e)
ATT_COLS = 128
I16 = jnp.int16
I16_MIN = -2 ** 15
SRCH_ROWS = 64
SRCH_W = 256
SRCH_KEYS = 2048


def _wide(x, n):
    return jnp.concatenate([x] * (n // LANES), axis=1)


def _dsa_prompt_kernel(q_ref, qi_ref, s_ref, k_ref, vt_ref, kit_ref, rb_ref, o_ref,
                       hi_s, lo_s, wb_s, bias_s, m_s, l_s, acc_s, j_s, mb_s, cand_s, thi_s, tlo_s, cnt_s,
                       *, tq, nbits, sk):
    b = pl.program_id(0)
    i = pl.program_id(1)
    row = lax.broadcasted_iota(I32, (tq, tq), 0)
    col = lax.broadcasted_iota(I32, (tq, tq), 1)

    @pl.when((b == 0) & (i == 0))
    def _():
        for h in range(C_HEADS):
            bias_s[0, h] = _bias_table(col + tq - row, rb_ref, h) * LOG2E
            bias_s[1, h] = _bias_table(col - row, rb_ref, h) * LOG2E

    @pl.when(i == 0)
    def _():
        hi_s[...] = jnp.full(hi_s.shape, I16_MIN, I16)
        lo_s[...] = jnp.full(lo_s.shape, I16_MIN, I16)

    wscale = IDX_DIM ** -0.5 * IDX_HEADS ** -0.5
    for h in range(IDX_HEADS):
        wcol = s_ref[:, IDX_DIM + h:IDX_DIM + h + 1] * wscale
        wb_s[h] = jnp.broadcast_to(wcol, (tq, LANES))

    def score_tile(j, carry):
        off = pl.multiple_of(j * tq, tq)
        kt = kit_ref[0, :, pl.ds(off, tq)]
        sc = jnp.zeros((tq, tq), F32)
        for h in range(IDX_HEADS):
            d = _dot(qi_ref[:, h * LANES:(h + 1) * LANES], kt)
            sc = sc + jnp.maximum(d, 0.0) * _wide(wb_s[h], tq)
        sc = jnp.where((j == i) & (col > row), -jnp.inf, sc)
        key = _order_key(sc)
        hi_s[:, pl.ds(off, tq)] = lax.shift_right_arithmetic(key, 16).astype(I16)
        lo_s[:, pl.ds(off, tq)] = ((key & 0xFFFF) + I16_MIN).astype(I16)
        return carry

    lax.fori_loop(0, i + 1, score_tile, 0)

    n_steps = ((i + 1) * tq + sk - 1) // sk
    ones = jnp.ones((SRCH_W, LANES), BF16)

    def scan(fn, init=None):
        for rc in range(tq // SRCH_ROWS):
            rows = slice(rc * SRCH_ROWS, (rc + 1) * SRCH_ROWS)

            def body(j, acc):
                for q in range(sk // SRCH_W):
                    off = pl.multiple_of(j * sk, sk) + q * SRCH_W
                    acc = fn(rows, off, acc)
                return acc

            out = lax.fori_loop(0, n_steps, body, init)
            if init is not None:
                cnt_s[rows, :] = out

    def count(buf, pred):
        def fn(rows, off, acc):
            return acc + jnp.where(pred(buf[rows, pl.ds(off, SRCH_W)], cand_s[rows, :]), BF16(1), BF16(0))
        scan(fn, jnp.zeros((SRCH_ROWS, SRCH_W), BF16))
        return _dot(cnt_s[...], ones)

    def search(buf, base, n_init):
        def bit_step(bi, carry):
            thr, n_ge = carry
            cand = thr + lax.shift_left(jnp.int32(1), 15 - bi)
            cand_s[...] = _wide(cand.astype(I16), SRCH_W)
            cnt = count(buf, lambda kq, c: kq >= c) + base
            ok = cnt >= TOPK
            return jnp.where(ok, cand, thr), jnp.where(ok, cnt, n_ge)
        return lax.fori_loop(0, 16, bit_step, (jnp.full((tq, LANES), I16_MIN, I32), n_init))

    n_all = jnp.zeros((tq, LANES), F32) + ((i + 1) * tq).astype(F32)
    thi, n_ge_hi = search(hi_s, 0.0, n_all)
    thi_s[...] = _wide(thi.astype(I16), SRCH_W)
    cand_s[...] = thi_s[...]
    n_gt_hi = count(hi_s, lambda kq, c: kq > c)

    def keep_tied_hi(rows, off, acc):
        tied = hi_s[rows, pl.ds(off, SRCH_W)] == thi_s[rows, :]
        lo_s[rows, pl.ds(off, SRCH_W)] = jnp.where(tied, lo_s[rows, pl.ds(off, SRCH_W)], I16(I16_MIN))
        return acc

    scan(keep_tied_hi)
    tlo, n_ge = search(lo_s, n_gt_hi, n_ge_hi)
    tlo_s[...] = _wide(tlo.astype(I16), SRCH_W)
    cand_s[...] = tlo_s[...]
    n_gt = count(lo_s, lambda kq, c: kq > c) + n_gt_hi

    need = TOPK - n_gt
    j_s[...] = jnp.full((tq, SRCH_W), 2 ** 14, I16)
    any_tie = jnp.max(jnp.where(n_ge > TOPK, 1.0, 0.0)) > 0.0

    @pl.when(any_tie)
    def _():
        lane16 = lax.broadcasted_iota(I32, (SRCH_ROWS, SRCH_W), 1)

        def tied_below(rows, off, acc):
            tied = ((hi_s[rows, pl.ds(off, SRCH_W)] == thi_s[rows, :])
                    & (lo_s[rows, pl.ds(off, SRCH_W)] == tlo_s[rows, :])
                    & ((lane16 + off).astype(I16) < cand_s[rows, :]))
            return acc + jnp.where(tied, BF16(1), BF16(0))

        def idx_step(bi, jv):
            cand = jv + lax.shift_left(jnp.int32(1), nbits - 1 - bi)
            cand_s[...] = _wide(cand.astype(I16), SRCH_W)
            scan(tied_below, jnp.zeros((SRCH_ROWS, SRCH_W), BF16))
            cnt = _dot(cnt_s[...], ones)
            return jnp.where(cnt < need, cand, jv)

        jv = lax.fori_loop(0, nbits, idx_step, jnp.zeros((tq, LANES), I32))
        j_s[...] = _wide(jv.astype(I16), SRCH_W)

    m_s[...] = jnp.full(m_s.shape, NEG_BIG, F32)
    l_s[...] = jnp.zeros(l_s.shape, F32)
    acc_s[...] = jnp.zeros(acc_s.shape, F32)
    rowg16 = (row + i * tq).astype(I16)

    def att_tile(j, tab):
        off = pl.multiple_of(j * tq, tq)
        kh16 = hi_s[:, pl.ds(off, tq)]
        kl16 = lo_s[:, pl.ds(off, tq)]
        colg16 = (col + off).astype(I16)
        thi16, tlo16 = thi_s[...], tlo_s[...]
        mask = ((kh16 > thi16) | ((kh16 == thi16) & ((kl16 > tlo16) | ((kl16 == tlo16) & (colg16 <= j_s[...])))))
        mask = mask & (colg16 <= rowg16)
        mb_s[...] = (jnp.where(mask, I16(0), I16(-1)).astype(F32) * (-NEG_BIG)).T
        for h in range(C_HEADS):
            hs = slice(h * C_HEAD_DIM, (h + 1) * C_HEAD_DIM)
            kh = k_ref[0, pl.ds(off, tq), hs]
            vth = vt_ref[0, hs, pl.ds(off, tq)]
            for qc in range(tq // ATT_COLS):
                qs = slice(qc * ATT_COLS, (qc + 1) * ATT_COLS)
                s = _dot_nt(kh, q_ref[qs, hs]) + mb_s[:, qs]
                if tab is not None:
                    s = s + bias_s[tab, h, :, qs]
                m_prev = m_s[h, :, qs]
                m_new = jnp.maximum(m_prev, jnp.max(s, axis=0, keepdims=True))
                alpha = jnp.exp2(m_prev - m_new)
                p = jnp.exp2(s - m_new)
                l_s[h, :, qs] = alpha * l_s[h, :, qs] + jnp.sum(p, axis=0, keepdims=True)
                acc_s[h, :, qs] = alpha * acc_s[h, :, qs] + _dot(vth, p.astype(BF16))
                m_s[h, :, qs] = m_new

    def far_tile(j, carry):
        att_tile(j, None)
        return carry

    lax.fori_loop(0, jnp.maximum(i - 1, 0), far_tile, 0)

    @pl.when(i >= 1)
    def _():
        att_tile(i - 1, 0)

    att_tile(i, 1)
    for h in range(C_HEADS):
        o_ref[:, h * C_HEAD_DIM:(h + 1) * C_HEAD_DIM] = (acc_s[h] / l_s[h]).T.astype(o_ref.dtype)


def dsa_prompt(q16, qi16, s32, k16, vt16, kit16, rel_bias, *, n_seq, tq=256):
    m = q16.shape[0]
    seq = m // n_seq
    nq = seq // tq
    nbits = int(math.log2(seq))
    assert 2 ** nbits == seq and seq % tq == 0
    assert tq == SRCH_W and tq >= TOPK
    kern = functools.partial(_dsa_prompt_kernel, tq=tq, nbits=nbits, sk=min(SRCH_KEYS, seq))
    once = pl.Buffered(1)
    return pl.pallas_call(
        kern,
        grid=(n_seq, nq),
        in_specs=[pl.BlockSpec((tq, C_WIDTH), lambda b, i: (b * nq + i, 0)),
                  pl.BlockSpec((tq, IDX_HEADS * LANES), lambda b, i: (b * nq + i, 0)),
                  pl.BlockSpec((tq, LANES), lambda b, i: (b * nq + i, 0)),
                  pl.BlockSpec((1, seq, C_WIDTH), lambda b, i: (b, 0, 0), pipeline_mode=once),
                  pl.BlockSpec((1, C_WIDTH, seq), lambda b, i: (b, 0, 0), pipeline_mode=once),
                  pl.BlockSpec((1, LANES, seq), lambda b, i: (b, 0, 0), pipeline_mode=once),
                  pl.BlockSpec(memory_space=pltpu.SMEM)],
        out_specs=pl.BlockSpec((tq, C_WIDTH), lambda b, i: (b * nq + i, 0)),
        out_shape=jax.ShapeDtypeStruct((m, C_WIDTH), BF16),
        scratch_shapes=[pltpu.VMEM((tq, seq), I16),
                        pltpu.VMEM((tq, seq), I16),
                        pltpu.VMEM((IDX_HEADS, tq, LANES), F32),
                        pltpu.VMEM((2, C_HEADS, tq, tq), F32),
                        pltpu.VMEM((C_HEADS, 1, tq), F32),
                        pltpu.VMEM((C_HEADS, 1, tq), F32),
                        pltpu.VMEM((C_HEADS, C_HEAD_DIM, tq), F32),
                        pltpu.VMEM((tq, SRCH_W), I16),
                        pltpu.VMEM((tq, tq), F32),
                        pltpu.VMEM((tq, SRCH_W), I16),
                        pltpu.VMEM((tq, SRCH_W), I16),
                        pltpu.VMEM((tq, SRCH_W), I16),
                        pltpu.VMEM((tq, SRCH_W), BF16)],
        compiler_params=_cparams(("arbitrary", "arbitrary")),
        name="dsa_prompt",
    )(q16, qi16, s32, k16, vt16, kit16, rel_bias)


NEW_ROWS = 16


def _dsa_sample_kernel(pt_ref, *refs, n_pages, t_new):
    ki_refs = refs[0:n_pages]
    k_refs = refs[n_pages:2 * n_pages]
    v_refs = refs[2 * n_pages:3 * n_pages]
    (qi_ref, wrep_ref, q_ref, kin_ref, knew_ref, vnew_ref, rb_ref, o_ref,
     kiall_s, xk_s, xv_s, lg_s, tab_s, exp_s) = refs[3 * n_pages:]
    b = pl.program_id(0)
    past = n_pages * PAGE
    nkeys = past + PAGE
    n_tiles = n_pages + 1
    nth = t_new * C_HEADS
    pw = PAGE * C_HEADS
    rowx = lax.broadcasted_iota(I32, (nth, pw), 0)
    lanex = lax.broadcasted_iota(I32, (nth, pw), 1)
    own_head = (lanex % C_HEADS) == (rowx % C_HEADS)

    @pl.when(b == 0)
    def _():
        kiall_s[IDX_DIM:LANES, :] = jnp.zeros((LANES - IDX_DIM, nkeys), BF16)
        xk_s[past * C_HEADS:nkeys * C_HEADS, :] = jnp.zeros((pw, C_HEAD_DIM), BF16)
        xv_s[past * C_HEADS:nkeys * C_HEADS, :] = jnp.zeros((pw, C_HEAD_DIM), BF16)
        ei = lax.broadcasted_iota(I32, (PAGE, pw), 0)
        ej = lax.broadcasted_iota(I32, (PAGE, pw), 1)
        exp_s[...] = jnp.where(ej // C_HEADS == ei, 1.0, 0.0).astype(BF16)
        t_of = rowx // C_HEADS
        k_of = lanex // C_HEADS
        tab0 = jnp.zeros((nth, pw), F32)
        tab1 = jnp.zeros((nth, pw), F32)
        for h in range(C_HEADS):
            sel = (lanex % C_HEADS) == h
            tab0 = jnp.where(sel, _bias_table(PAGE + t_of - k_of, rb_ref, h), tab0)
            tab1 = jnp.where(sel, _bias_table(t_of - k_of, rb_ref, h), tab1)
        tab_s[0] = tab0 * LOG2E
        tab_s[1] = tab1 * LOG2E

    for p in range(n_pages):
        kiall_s[0:IDX_DIM, p * PAGE:(p + 1) * PAGE] = ki_refs[p][0, 0].astype(BF16)
        xk_s[p * pw:(p + 1) * pw, :] = k_refs[p][0, 0].reshape(pw, C_HEAD_DIM).astype(BF16)
        xv_s[p * pw:(p + 1) * pw, :] = v_refs[p][0, 0].reshape(pw, C_HEAD_DIM).astype(BF16)
    kiall_s[0:IDX_DIM, past:nkeys] = kin_ref[0].astype(BF16)
    xk_s[past * C_HEADS:(past + NEW_ROWS) * C_HEADS, :] = knew_ref[0]
    xv_s[past * C_HEADS:(past + NEW_ROWS) * C_HEADS, :] = vnew_ref[0]

    d = _dot(qi_ref[0], kiall_s[...])
    r = jnp.maximum(d, 0.0) * _wide(wrep_ref[0], nkeys)
    sc = jnp.sum(r.reshape(t_new, C_HEADS, nkeys), axis=1)
    rowr = lax.broadcasted_iota(I32, (t_new, nkeys), 0)
    colr = lax.broadcasted_iota(I32, (t_new, nkeys), 1)
    valid = (colr < past) | (colr - past <= rowr)
    kr = _order_key(jnp.where(valid, sc, -jnp.inf))

    def pick(ok1, c1, ok2, c2, ok3, c3, cur):
        return jnp.where(ok3, c3, jnp.where(ok2, c2, jnp.where(ok1, c1, cur)))

    def wrap32(v):
        return ((v + 2 ** 31) % 2 ** 32) - 2 ** 31

    thr = jnp.full((t_new, 1), INT_MIN, I32)
    for bi in range(16):
        one = 1 << (30 - 2 * bi)
        c1, c2, c3 = thr + one, thr + wrap32(2 * one), thr + wrap32(3 * one)
        cnt = lambda c: jnp.sum(jnp.where(kr >= c, 1, 0), axis=1, keepdims=True)
        thr = pick(cnt(c1) >= TOPK, c1, cnt(c2) >= TOPK, c2, cnt(c3) >= TOPK, c3, thr)
    need = TOPK - jnp.sum(jnp.where(kr > thr, 1, 0), axis=1, keepdims=True)
    tie = kr == thr
    nsteps = (int(math.ceil(math.log2(nkeys))) + 1) // 2
    jv = jnp.zeros((t_new, 1), I32)
    for bi in range(nsteps):
        one = 1 << (2 * (nsteps - 1 - bi))
        c1, c2, c3 = jv + one, jv + 2 * one, jv + 3 * one
        cnt = lambda c: jnp.sum(jnp.where(tie & (colr < c), 1, 0), axis=1, keepdims=True)
        jv = pick(cnt(c1) < need, c1, cnt(c2) < need, c2, cnt(c3) < need, c3, jv)
    sel = ((kr > thr) | (tie & (colr <= jv))) & valid
    sel = jnp.broadcast_to(jnp.where(sel, 1.0, 0.0)[:, None, :], (t_new, C_HEADS, nkeys))
    sel16 = sel.reshape(nth, nkeys).astype(BF16)

    qs = q_ref[0]
    sel_rows = jnp.concatenate([sel16[:, p * PAGE:(p + 1) * PAGE] for p in range(n_tiles)], axis=0)
    selx = _dot(sel_rows, exp_s[...])
    for p in range(n_tiles):
        s = _dot_nt(qs, xk_s[p * pw:(p + 1) * pw, :])
        if p >= n_pages - 1:
            s = s + tab_s[p - (n_pages - 1)]
        lg_s[p] = jnp.where((selx[p * nth:(p + 1) * nth, :] > 0.5) & own_head, s, NEG_BIG)
    mx = jnp.full((nth, 1), NEG_BIG, F32)
    for p in range(n_tiles):
        mx = jnp.maximum(mx, jnp.max(lg_s[p], axis=1, keepdims=True))
    den = jnp.zeros((nth, 1), F32)
    out = jnp.zeros((nth, C_HEAD_DIM), F32)
    for p in range(n_tiles):
        pr = jnp.exp2(lg_s[p] - mx)
        den = den + jnp.sum(pr, axis=1, keepdims=True)
        out = out + _dot(pr.astype(BF16), xv_s[p * pw:(p + 1) * pw, :])
    o_ref[0] = out / den


def sample_dsa_operands(q16, qi16, s32, k16, v16, t_new, db):
    nth = t_new * C_HEADS
    by_seq = lambda a: a.reshape((t_new, db) + a.shape[1:]).swapaxes(0, 1)
    qi = by_seq(qi16.reshape(t_new * db, IDX_HEADS, LANES)).reshape(db, nth, LANES)
    q = by_seq(q16.reshape(t_new * db, C_HEADS, C_HEAD_DIM)).reshape(db, nth, C_HEAD_DIM)
    wscale = IDX_DIM ** -0.5 * IDX_HEADS ** -0.5
    w = by_seq(s32[:, IDX_DIM:IDX_DIM + IDX_HEADS]).reshape(db, nth, 1) * wscale
    wrep = jnp.broadcast_to(w, (db, nth, LANES))
    row_pad = ((0, 0), (0, NEW_ROWS - t_new), (0, 0))
    kin = jnp.pad(by_seq(s32[:, :IDX_DIM]).swapaxes(1, 2), ((0, 0), (0, 0), (0, PAGE - t_new)))
    new_rows = lambda a: jnp.pad(by_seq(a), row_pad).reshape(db, NEW_ROWS * C_HEADS, C_HEAD_DIM)
    return qi, wrep, q, kin, new_rows(k16), new_rows(v16)


def dsa_sample(page_table, layer, pool_ki, pool_k, pool_v, qi16, wrep, q16, kin, knew16, vnew16, rel_bias, *, t_new):
    db, n_pages = page_table.shape
    n_tiles = n_pages + 1
    nkeys = n_tiles * PAGE
    nth = t_new * C_HEADS
    pw = PAGE * C_HEADS
    kern = functools.partial(_dsa_sample_kernel, n_pages=n_pages, t_new=t_new)

    def ki_spec(p):
        return pl.BlockSpec((1, 1, IDX_DIM, PAGE), lambda b, pt: (layer, pt[b, p], 0, 0))

    def kv_spec(p):
        return pl.BlockSpec((1, 1, PAGE, C_HEADS, C_HEAD_DIM), lambda b, pt: (layer, pt[b, p], 0, 0, 0))

    per_seq = lambda shape: pl.BlockSpec((1,) + shape, lambda b, pt: (b, 0, 0))
    in_specs = ([ki_spec(p) for p in range(n_pages)]
                + [kv_spec(p) for p in range(n_pages)]
                + [kv_spec(p) for p in range(n_pages)]
                + [per_seq((nth, LANES)), per_seq((nth, LANES)), per_seq((nth, C_HEAD_DIM)),
                   per_seq((IDX_DIM, PAGE)), per_seq((NEW_ROWS * C_HEADS, C_HEAD_DIM)),
                   per_seq((NEW_ROWS * C_HEADS, C_HEAD_DIM)),
                   pl.BlockSpec(memory_space=pltpu.SMEM)])
    grid_spec = pltpu.PrefetchScalarGridSpec(
        num_scalar_prefetch=1,
        grid=(db,),
        in_specs=in_specs,
        out_specs=pl.BlockSpec((1, nth, C_HEAD_DIM), lambda b, pt: (b, 0, 0)),
        scratch_shapes=[pltpu.VMEM((LANES, nkeys), BF16),
                        pltpu.VMEM((nkeys * C_HEADS, C_HEAD_DIM), BF16),
                        pltpu.VMEM((nkeys * C_HEADS, C_HEAD_DIM), BF16),
                        pltpu.VMEM((n_tiles, nth, pw), F32),
                        pltpu.VMEM((2, nth, pw), F32),
                        pltpu.VMEM((PAGE, pw), BF16)])
    return pl.pallas_call(
        kern,
        grid_spec=grid_spec,
        out_shape=jax.ShapeDtypeStruct((db, nth, C_HEAD_DIM), F32),
        compiler_params=_cparams(("arbitrary",)),
        name="dsa_sample",
    )(page_table, *([pool_ki] * n_pages), *([pool_k] * n_pages), *([pool_v] * n_pages),
      qi16, wrep, q16, kin, knew16, vnew16, rel_bias)


PROMPT_TM = 512
FFN_TM = 512
ODD_COLS = 3 * C_WIDTH + IDX_HEADS * LANES + LANES


def _odd_weight(w):
    d = w.shape[0]
    qkv = w[:, :3 * C_WIDTH]
    qi = w[:, 3 * C_WIDTH:3 * C_WIDTH + IDX_HEADS * IDX_DIM].reshape(d, IDX_HEADS, IDX_DIM)
    qi = jnp.pad(qi, ((0, 0), (0, 0), (0, LANES - IDX_DIM))).reshape(d, IDX_HEADS * LANES)
    tail = w[:, 3 * C_WIDTH + IDX_HEADS * IDX_DIM:]
    tail = jnp.pad(tail, ((0, 0), (0, LANES - tail.shape[1])))
    return jnp.concatenate([qkv, qi, tail], axis=1).astype(BF16)


ODD_SEGS = ((0, C_WIDTH, C_HEAD_DIM ** -0.5 * LOG2E, (0,)),
            (C_WIDTH, C_WIDTH, 1.0, (1, 2)),
            (2 * C_WIDTH, C_WIDTH, 1.0, (3, 4)),
            (3 * C_WIDTH, IDX_HEADS * LANES, 1.0, (5,)),
            (3 * C_WIDTH + IDX_HEADS * LANES, LANES, 1.0, (6,)))
ODD_OUTS = ((C_WIDTH, BF16), (C_WIDTH, F32), (C_WIDTH, BF16), (C_WIDTH, F32), (C_WIDTH, BF16),
            (IDX_HEADS * LANES, BF16), (LANES, F32))
EVEN_COLS = 2 * A_WIDTH + 4 * B_WIDTH


def kernel(x_prompt, x_sample, state_conv_a, state_hgrn, cache_k, cache_v, cache_ki, state_ffn_conv, page_table, norm_mix, norm_ffn, norm_final, w_in_even, w_in_odd, w_out, conv_a_w, conv_a_b, ln_a_g, ln_a_b, hgrn_lb, hgrn_gn, rel_bias, w_up, ffn_conv_w, ffn_conv_b, w_down):
    bsz, seq, d = x_prompt.shape
    db, t_new, _ = x_sample.shape
    depth = norm_mix.shape[0]
    n_phys = cache_k.shape[1]
    to_tm = lambda a: jnp.transpose(a, (1, 0, 2))

    xp = x_prompt.reshape(bsz * seq, d)
    xs = to_tm(x_sample).reshape(t_new * db, d)
    lbs = jnp.cumsum(jax.nn.softmax(hgrn_lb.astype(F32), axis=0), axis=0)
    lbs = lbs - lbs[0:1]
    log_lb = jnp.log(lbs)
    log_1m_lb = jnp.log1p(-lbs)
    tiles_per_seq = seq // FFN_TM

    p_conv, p_hg, p_k, p_v, p_ki, p_ffn = [], [], [], [], [], []
    s_conv, s_hg, s_k, s_v, s_ki, s_ffn = [], [], [], [], [], []
    for l in range(depth):
        j = l // 2
        wo16 = w_out[l].astype(BF16)
        if l % 2 == 0:
            w16 = w_in_even[j].astype(BF16)
            segs = ((0, EVEN_COLS, 1.0, (0,)),)
            (zp,) = norm_matmul(xp, norm_mix[l], w16, segs, ((EVEN_COLS, F32),), tm=PROMPT_TM)
            (zs,) = norm_matmul(xs, norm_mix[l], w16, segs, ((EVEN_COLS, F32),), tm=PROMPT_TM)
            conv = (conv_a_w[j], conv_a_b[j], ln_a_g[j], ln_a_b[j])
            ap, cp = mixer_a(zp, *conv, jnp.zeros((bsz, A_CONV - 1, A_WIDTH), F32), n_seq=bsz, dil=1, tm=PROMPT_TM)
            bp, sp = hgrn(zp, 2, log_lb[j], log_1m_lb[j], hgrn_gn[j],
                          jnp.zeros((1, bsz, B_HEADS, B_DK, B_DK), F32), 0,
                          n_seq=bsz, rt=HG_CHUNK, n_real=HG_CHUNK, nsb=HG_CHUNK // HG_SUB)
            pre_t = to_tm(state_conv_a[j]).reshape(1, (A_CONV - 1) * db, A_WIDTH)
            a_s, cs = mixer_a(zs, *conv, pre_t, n_seq=1, dil=db, tm=t_new * db)
            zb = to_tm(zs.reshape(t_new, db, EVEN_COLS))[:, :, 2 * A_WIDTH:]
            zb = jnp.pad(zb, ((0, 0), (0, 8 - t_new), (0, 0))).reshape(db * 8, 4 * B_WIDTH)
            bs8, ss = hgrn(zb, 0, log_lb[j], log_1m_lb[j], hgrn_gn[j], state_hgrn, j,
                           n_seq=db, rt=8, n_real=t_new, nsb=1)
            b_s = to_tm(bs8.reshape(db, 8, B_WIDTH)[:, :t_new]).reshape(t_new * db, B_WIDTH)
            xp = outproj(xp, [ap, bp], [wo16[:A_WIDTH], wo16[A_WIDTH:]], tm=PROMPT_TM)
            xs = outproj(xs, [a_s, b_s], [wo16[:A_WIDTH], wo16[A_WIDTH:]], tm=PROMPT_TM)
            p_conv.append(cp)
            p_hg.append(sp)
            s_conv.append(to_tm(cs.reshape(A_CONV - 1, db, A_WIDTH)))
            s_hg.append(ss)
        else:
            w16 = _odd_weight(w_in_odd[j])
            q16, k32, k16, v32, v16, qi16, s32 = norm_matmul(xp, norm_mix[l], w16, ODD_SEGS, ODD_OUTS, tm=PROMPT_TM)
            kit16 = jnp.pad(jnp.transpose(s32[:, :IDX_DIM].reshape(bsz, seq, IDX_DIM), (0, 2, 1)),
                            ((0, 0), (0, LANES - IDX_DIM), (0, 0))).astype(BF16)
            vt16 = jnp.transpose(v16.reshape(bsz, seq, C_WIDTH), (0, 2, 1))
            mp = dsa_prompt(q16, qi16, s32, k16.reshape(bsz, seq, C_WIDTH), vt16, kit16, rel_bias, n_seq=bsz)
            p_k.append(k32.reshape(bsz, seq, C_HEADS, C_HEAD_DIM))
            p_v.append(v32.reshape(bsz, seq, C_HEADS, C_HEAD_DIM))
            p_ki.append(s32[:, :IDX_DIM].reshape(bsz, seq, IDX_DIM))
            q16s, k32s, k16s, v32s, v16s, qi16s, s32s = norm_matmul(xs, norm_mix[l], w16, ODD_SEGS, ODD_OUTS,
                                                                   tm=PROMPT_TM)
            ops = sample_dsa_operands(q16s, qi16s, s32s, k16s, v16s, t_new, db)
            o_s = dsa_sample(page_table, j, jnp.swapaxes(cache_ki, 2, 3), cache_k, cache_v, *ops, rel_bias,
                             t_new=t_new)
            m_s = to_tm(o_s.reshape(db, t_new, C_WIDTH)).reshape(t_new * db, C_WIDTH)
            xp = outproj(xp, [mp], [wo16], tm=PROMPT_TM)
            xs = outproj(xs, [m_s], [wo16], tm=PROMPT_TM)
            s_k.append(to_tm(k32s.reshape(t_new, db, C_WIDTH)).reshape(db, t_new, C_HEADS, C_HEAD_DIM))
            s_v.append(to_tm(v32s.reshape(t_new, db, C_WIDTH)).reshape(db, t_new, C_HEADS, C_HEAD_DIM))
            s_ki.append(to_tm(s32s[:, :IDX_DIM].reshape(t_new, db, IDX_DIM)))
        ffn_w = (w_up[l].astype(BF16), ffn_conv_w[l], ffn_conv_b[l], w_down[l].astype(BF16))
        xp, fcp = ffn(xp, norm_ffn[l], *ffn_w, jnp.zeros((bsz, 2, 2 * D_FF), F32), n_seq=bsz, dil=1, tm=FFN_TM)
        pre_t = to_tm(state_ffn_conv[l]).reshape(1, 2 * db, 2 * D_FF)
        xs, fcs = ffn(xs, norm_ffn[l], *ffn_w, pre_t, n_seq=1, dil=db, tm=t_new * db)
        p_ffn.append(fcp[tiles_per_seq - 1::tiles_per_seq])
        s_ffn.append(to_tm(fcs.reshape(2, db, 2 * D_FF)))
    y_prompt = rmsnorm_call(xp, norm_final, tm=PROMPT_TM).reshape(bsz, seq, d)
    y_sample = to_tm(rmsnorm_call(xs, norm_final, tm=PROMPT_TM).reshape(t_new, db, d))
    return (y_prompt, y_sample,
            jnp.stack(p_conv), jnp.stack(p_hg), jnp.stack(p_k), jnp.stack(p_v), jnp.stack(p_ki), jnp.stack(p_ffn),
            jnp.stack(s_conv), jnp.stack(s_hg), jnp.stack(s_k), jnp.stack(s_v), jnp.stack(s_ki), jnp.stack(s_ffn))
```

```python
import functools
import math

import jax
import jax.numpy as jnp
from jax import lax
from jax.experimental import pallas as pl
from jax.experimental.pallas import tpu as pltpu

F32 = jnp.float32
BF16 = jnp.bfloat16
I32 = jnp.int32

EPS = 1e-6
D_MODEL = 1024
A_WIDTH = 512
A_CONV = 31
B_HEADS = 4
B_DK = 128
B_WIDTH = 512
C_HEADS = 8
C_HEAD_DIM = 128
C_WIDTH = 1024
IDX_HEADS = 8
IDX_DIM = 64
TOPK = 256
REL_BUCKETS = 32
REL_MAX_DIST = 128
D_FF = 2816
PAGE = 128

LANES = 128
VMEM_LIMIT = 60 * 1024 * 1024
NEG_BIG = -1e30
INT_MIN = -2 ** 31


def _cparams(sem):
    return pltpu.CompilerParams(dimension_semantics=sem, vmem_limit_bytes=VMEM_LIMIT)


def _rms(x, g):
    return x * lax.rsqrt(jnp.mean(x * x, axis=-1, keepdims=True) + EPS) * g


def _sigmoid(x):
    return 1.0 / (1.0 + jnp.exp(-x))


def _dot(a, b):
    return jnp.dot(a, b, preferred_element_type=F32)


def _dot_nt(a, b):
    return lax.dot_general(a, b, (((1,), (1,)), ((), ())), preferred_element_type=F32)


def _norm_matmul_kernel(x_ref, g_ref, w_ref, *out_refs, segs):
    hb = _rms(x_ref[...], g_ref[...]).astype(BF16)
    for start, width, scale, outs in segs:
        for c0 in range(0, width, 512):
            cw = min(512, width - c0)
            r = _dot(hb, w_ref[:, start + c0:start + c0 + cw])
            if scale != 1.0:
                r = r * scale
            for oi in outs:
                out_refs[oi][:, c0:c0 + cw] = r.astype(out_refs[oi].dtype)


def norm_matmul(x, g, w16, segs, out_defs, tm=512):
    m, d = x.shape
    tm = min(tm, m)
    n = w16.shape[1]
    return pl.pallas_call(
        functools.partial(_norm_matmul_kernel, segs=segs),
        grid=(m // tm,),
        in_specs=[pl.BlockSpec((tm, d), lambda i: (i, 0)),
                  pl.BlockSpec((1, d), lambda i: (0, 0)),
                  pl.BlockSpec((d, n), lambda i: (0, 0))],
        out_specs=[pl.BlockSpec((tm, wd), lambda i: (i, 0)) for wd, _ in out_defs],
        out_shape=[jax.ShapeDtypeStruct((m, wd), dt) for wd, dt in out_defs],
        compiler_params=_cparams(("arbitrary",)),
        name="norm_matmul",
    )(x, g.reshape(1, d), w16)


def _outproj_kernel(*refs, n_in):
    x_ref = refs[0]
    o_ref = refs[-1]
    acc = x_ref[...]
    for i in range(n_in):
        acc = acc + _dot(refs[1 + 2 * i][...].astype(BF16), refs[2 + 2 * i][...])
    o_ref[...] = acc


def outproj(x, ms, ws, tm=512):
    m, d = x.shape
    tm = min(tm, m)
    in_specs = [pl.BlockSpec((tm, d), lambda i: (i, 0))]
    args = [x]
    for mm, w in zip(ms, ws):
        in_specs.append(pl.BlockSpec((tm, mm.shape[1]), lambda i: (i, 0)))
        in_specs.append(pl.BlockSpec(w.shape, lambda i: (0, 0)))
        args += [mm, w]
    return pl.pallas_call(
        functools.partial(_outproj_kernel, n_in=len(ms)),
        grid=(m // tm,),
        in_specs=in_specs,
        out_specs=pl.BlockSpec((tm, d), lambda i: (i, 0)),
        out_shape=jax.ShapeDtypeStruct((m, d), F32),
        compiler_params=_cparams(("arbitrary",)),
        name="outproj",
    )(*args)


FFN_SUB = 256


def _ffn_kernel(x_ref, g_ref, wup_ref, cw_ref, cb_ref, wdn_ref, pre_ref, o_ref, npre_ref,
                ubuf_s, *, tm, dil, pad, tiles_per_seq):
    first = (pl.program_id(0) % tiles_per_seq) == 0

    @pl.when(first)
    def _():
        ubuf_s[pad - 2 * dil:pad, :] = pre_ref[0]

    sub = min(tm, FFN_SUB)
    for r in range(tm // sub):
        r0 = pad + r * sub
        xr = x_ref[r * sub:(r + 1) * sub, :]
        u = _dot(_rms(xr, g_ref[...]).astype(BF16), wup_ref[...])
        ubuf_s[r0:r0 + sub, :] = u
        uc = (u * cw_ref[2:3, :] + ubuf_s[r0 - dil:r0 - dil + sub, :] * cw_ref[1:2, :]
              + ubuf_s[r0 - 2 * dil:r0 - 2 * dil + sub, :] * cw_ref[0:1, :] + cb_ref[...])
        gate = uc[:, :D_FF]
        val = uc[:, D_FF:]
        act = (gate * _sigmoid(gate) * val).astype(BF16)
        o_ref[r * sub:(r + 1) * sub, :] = xr + _dot(act, wdn_ref[...])
    tail = ubuf_s[pad + tm - 2 * dil:pad + tm, :]
    npre_ref[0] = tail
    if tiles_per_seq > 1:
        ubuf_s[pad - 2 * dil:pad, :] = tail


def ffn(x, g, wup16, cw, cb, wdn16, pre, *, n_seq, dil, tm):
    m, d = x.shape
    f2 = 2 * D_FF
    tiles_per_seq = (m // n_seq) // tm
    pad = max(8, 2 * dil)
    kern = functools.partial(_ffn_kernel, tm=tm, dil=dil, pad=pad, tiles_per_seq=tiles_per_seq)
    once = pl.Buffered(1)
    return pl.pallas_call(
        kern,
        grid=(m // tm,),
        in_specs=[pl.BlockSpec((tm, d), lambda i: (i, 0)),
                  pl.BlockSpec((1, d), lambda i: (0, 0)),
                  pl.BlockSpec((d, f2), lambda i: (0, 0), pipeline_mode=once),
                  pl.BlockSpec((3, f2), lambda i: (0, 0)),
                  pl.BlockSpec((1, f2), lambda i: (0, 0)),
                  pl.BlockSpec((D_FF, d), lambda i: (0, 0), pipeline_mode=once),
                  pl.BlockSpec((1, 2 * dil, f2), lambda i: (i // tiles_per_seq, 0, 0))],
        out_specs=[pl.BlockSpec((tm, d), lambda i: (i, 0)),
                   pl.BlockSpec((1, 2 * dil, f2), lambda i: (i, 0, 0))],
        out_shape=[jax.ShapeDtypeStruct((m, d), F32),
                   jax.ShapeDtypeStruct((m // tm, 2 * dil, f2), F32)],
        scratch_shapes=[pltpu.VMEM((pad + tm, f2), F32)],
        compiler_params=_cparams(("arbitrary",)),
        name="conv_ffn",
    )(x, g.reshape(1, d), wup16, cw, cb.reshape(1, f2), wdn16, pre)


def _rmsnorm_kernel(x_ref, g_ref, o_ref):
    o_ref[...] = _rms(x_ref[...], g_ref[...])


def rmsnorm_call(x, g, tm=512):
    m, d = x.shape
    tm = min(tm, m)
    return pl.pallas_call(
        _rmsnorm_kernel,
        grid=(m // tm,),
        in_specs=[pl.BlockSpec((tm, d), lambda i: (i, 0)), pl.BlockSpec((1, d), lambda i: (0, 0))],
        out_specs=pl.BlockSpec((tm, d), lambda i: (i, 0)),
        out_shape=jax.ShapeDtypeStruct((m, d), F32),
        compiler_params=_cparams(("arbitrary",)),
        name="final_rmsnorm",
    )(x, g.reshape(1, d))


def _mixer_a_kernel(val_ref, gate_ref, cw_ref, cb_ref, lg_ref, lb_ref, pre_ref, o_ref, npre_ref,
                    gbuf_s, *, tm, dil, pad, tiles_per_seq):
    i = pl.program_id(0)
    first = (i % tiles_per_seq) == 0
    hist = (A_CONV - 1) * dil

    @pl.when(first)
    def _():
        gbuf_s[pad - hist:pad, :] = pre_ref[0]

    glu = val_ref[...] * _sigmoid(gate_ref[...])
    gbuf_s[pad:pad + tm, :] = glu
    npre_ref[0] = gbuf_s[pad + tm - hist:pad + tm, :]
    acc = jnp.zeros((tm, A_WIDTH), F32) + cb_ref[...]
    for j in range(A_CONV):
        s0 = pad - hist + j * dil
        acc = acc + gbuf_s[s0:s0 + tm, :] * cw_ref[j:j + 1, :]
    mu = jnp.mean(acc, axis=-1, keepdims=True)
    xc = acc - mu
    var = jnp.mean(xc * xc, axis=-1, keepdims=True)
    y = xc * lax.rsqrt(var + EPS) * lg_ref[...] + lb_ref[...]
    o_ref[...] = (y * _sigmoid(y)).astype(o_ref.dtype)
    if tiles_per_seq > 1:
        gbuf_s[0:pad, :] = gbuf_s[tm:tm + pad, :]


def mixer_a(z, cw, cb, lg, lb, pre, *, n_seq, dil, tm):
    m = z.shape[0]
    tiles_per_seq = (m // n_seq) // tm
    hist = (A_CONV - 1) * dil
    pad = 32 if dil == 1 else hist
    kern = functools.partial(_mixer_a_kernel, tm=tm, dil=dil, pad=pad, tiles_per_seq=tiles_per_seq)
    vec = lambda a: a.reshape(1, A_WIDTH)
    return pl.pallas_call(
        kern,
        grid=(m // tm,),
        in_specs=[pl.BlockSpec((tm, A_WIDTH), lambda i: (i, 0)),
                  pl.BlockSpec((tm, A_WIDTH), lambda i: (i, 1)),
                  pl.BlockSpec((A_CONV, A_WIDTH), lambda i: (0, 0)),
                  pl.BlockSpec((1, A_WIDTH), lambda i: (0, 0)),
                  pl.BlockSpec((1, A_WIDTH), lambda i: (0, 0)),
                  pl.BlockSpec((1, A_WIDTH), lambda i: (0, 0)),
                  pl.BlockSpec((1, hist, A_WIDTH), lambda i: (i // tiles_per_seq, 0, 0))],
        out_specs=[pl.BlockSpec((tm, A_WIDTH), lambda i: (i, 0)),
                   pl.BlockSpec((1, hist, A_WIDTH), lambda i: (i // tiles_per_seq, 0, 0))],
        out_shape=[jax.ShapeDtypeStruct((m, A_WIDTH), BF16),
                   jax.ShapeDtypeStruct((n_seq, hist, A_WIDTH), F32)],
        scratch_shapes=[pltpu.VMEM((pad + tm, A_WIDTH), F32)],
        compiler_params=_cparams(("arbitrary",)),
        name="mixer_a",
    )(z, z, cw, vec(cb), vec(lg), vec(lb), pre)


HG_CHUNK = 128
HG_SUB = 16


def _split3(x):
    hi = x.astype(BF16)
    r1 = x - hi.astype(F32)
    mid = r1.astype(BF16)
    lo = (r1 - mid.astype(F32)).astype(BF16)
    return hi, mid, lo


def _hgrn_chunk(qr, fr, ir, gr, llb, l1m, gn, st_s, b_s, q_s, k_s, v_s, *, n_real, nsb):
    c = HG_CHUNK
    q = qr * _sigmoid(qr)
    ls = jnp.minimum(fr, 0.0) - jnp.log1p(jnp.exp(-jnp.abs(fr)))
    bt = l1m + ls
    logf = jnp.maximum(llb, bt) + jnp.log1p(jnp.exp(-jnp.abs(llb - bt)))
    kk = 1.0 - jnp.exp(logf)
    if n_real < c:
        real = lax.broadcasted_iota(I32, (c, B_WIDTH), 0) < n_real
        logf = jnp.where(real, logf, 0.0)
        kk = jnp.where(real, kk, 0.0)
    ri = lax.broadcasted_iota(I32, (c, c), 0)
    ci = lax.broadcasted_iota(I32, (c, c), 1)
    tril = jnp.where(ci <= ri, 1.0, 0.0).astype(BF16)
    hi, mid, lo = _split3(logf)
    b = _dot(tril, hi) + _dot(tril, mid) + _dot(tril, lo)
    b_s[...] = b
    q_s[...] = q
    k_s[...] = kk
    v_s[...] = ir
    sub = HG_SUB
    row16 = lax.broadcasted_iota(I32, (sub, 1), 0)
    outs = []
    for h in range(B_HEADS):
        sl = slice(h * B_DK, (h + 1) * B_DK)
        st = st_s[h]
        o_carry = _dot_nt((q[:, sl] * jnp.exp(b[:, sl])).astype(BF16), st.astype(BF16))
        o_rows = []
        for i in range(nsb):
            r0 = i * sub
            qb = q_s[r0:r0 + sub, sl]
            bb = b_s[r0:r0 + sub, sl]
            o_blk = jnp.zeros((sub, B_DK), F32)
            if i > 0:
                ref_b = b_s[r0 - 1:r0, sl]
                qt = (qb * jnp.exp(bb - ref_b)).astype(BF16)
                kt = (k_s[0:r0, sl] * jnp.exp(ref_b - b_s[0:r0, sl])).astype(BF16)
                att = _dot_nt(qt, kt)
                o_blk = o_blk + _dot(att.astype(BF16), v_s[0:r0, sl].astype(BF16))
            for s in range(sub):
                r = r0 + s
                if r >= n_real:
                    continue
                p = qb * jnp.exp(bb - b_s[r:r + 1, sl]) * k_s[r:r + 1, sl]
                a = jnp.sum(p, axis=1, keepdims=True)
                a = jnp.where(row16 >= s, a, 0.0)
                o_blk = o_blk + a * v_s[r:r + 1, sl]
            o_rows.append(o_carry[r0:r0 + sub, :] + o_blk)
        if nsb * sub < c:
            o_rows.append(o_carry[nsb * sub:, :])
        o = jnp.concatenate(o_rows, axis=0)
        on = o * lax.rsqrt(jnp.mean(o * o, axis=-1, keepdims=True) + EPS) * gn
        g = gr[:, sl]
        outs.append(on * (g * _sigmoid(g)))
        bl = b_s[c - 1:c, sl]
        kd = (k_s[:, sl] * jnp.exp(bl - b_s[:, sl])).astype(BF16)
        vt = v_s[:, sl].T.astype(BF16)
        st_s[h] = st * jnp.exp(bl) + _dot(vt, kd)
    return jnp.concatenate(outs, axis=1)


def _hgrn_kernel(q_ref, f_ref, i_ref, g_ref, llb_ref, l1m_ref, gn_ref, s0_ref, o_ref, sout_ref,
                 st_s, b_s, q_s, k_s, v_s, *, rt, n_real, nsb, tiles_per_seq):
    c = HG_CHUNK
    first = (pl.program_id(0) % tiles_per_seq) == 0

    @pl.when(first)
    def _():
        for h in range(B_HEADS):
            st_s[h] = s0_ref[0, 0, h].T

    rows = min(rt, c)
    for ch in range(max(1, rt // c)):
        rs = slice(ch * c, ch * c + rows)

        def load(ref):
            x = ref[rs, :]
            if rows < c:
                x = jnp.concatenate([x, jnp.zeros((c - rows, B_WIDTH), F32)], axis=0)
            return x

        out = _hgrn_chunk(load(q_ref), load(f_ref), load(i_ref), load(g_ref),
                          llb_ref[...], l1m_ref[...], gn_ref[...],
                          st_s, b_s, q_s, k_s, v_s, n_real=n_real, nsb=nsb)
        o_ref[rs, :] = out[0:rows, :].astype(o_ref.dtype)
    for h in range(B_HEADS):
        sout_ref[0, h] = st_s[h].T


def hgrn(z, col0, llb, l1m, gn, s0, layer, *, n_seq, rt, n_real, nsb):
    m = z.shape[0]
    tiles_per_seq = (m // n_seq) // rt
    kern = functools.partial(_hgrn_kernel, rt=rt, n_real=n_real, nsb=nsb, tiles_per_seq=tiles_per_seq)
    c = HG_CHUNK
    zspec = lambda k: pl.BlockSpec((rt, B_WIDTH), lambda i: (i, col0 + k))
    state_spec = pl.BlockSpec((1, B_HEADS, B_DK, B_DK), lambda i: (i // tiles_per_seq, 0, 0, 0))
    return pl.pallas_call(
        kern,
        grid=(m // rt,),
        in_specs=[zspec(0), zspec(1), zspec(2), zspec(3),
                  pl.BlockSpec((1, B_WIDTH), lambda i: (0, 0)),
                  pl.BlockSpec((1, B_WIDTH), lambda i: (0, 0)),
                  pl.BlockSpec((1, B_DK), lambda i: (0, 0)),
                  pl.BlockSpec((1, 1, B_HEADS, B_DK, B_DK), lambda i: (layer, i // tiles_per_seq, 0, 0, 0))],
        out_specs=[pl.BlockSpec((rt, B_WIDTH), lambda i: (i, 0)), state_spec],
        out_shape=[jax.ShapeDtypeStruct((m, B_WIDTH), BF16),
                   jax.ShapeDtypeStruct((n_seq, B_HEADS, B_DK, B_DK), F32)],
        scratch_shapes=[pltpu.VMEM((B_HEADS, B_DK, B_DK), F32),
                        pltpu.VMEM((c, B_WIDTH), F32),
                        pltpu.VMEM((c, B_WIDTH), F32),
                        pltpu.VMEM((c, B_WIDTH), F32),
                        pltpu.VMEM((c, B_WIDTH), F32)],
        compiler_params=_cparams(("arbitrary",)),
        name="hgrn2",
    )(z, z, z, z, llb.reshape(1, B_WIDTH), l1m.reshape(1, B_WIDTH), gn.reshape(1, B_DK), s0)


def _order_key(x):
    b = lax.bitcast_convert_type(x + 0.0, I32)
    return jnp.where(b < 0, b ^ 0x7FFFFFFF, b)


def _t5_bucket(dist):
    n = jnp.maximum(dist, 0)
    max_exact = REL_BUCKETS // 2
    nf = jnp.maximum(n, 1).astype(F32)
    large = max_exact + (jnp.log(nf / max_exact) / math.log(REL_MAX_DIST / max_exact)
                         * (REL_BUCKETS - max_exact)).astype(I32)
    large = jnp.minimum(large, REL_BUCKETS - 1)
    return jnp.where(n < max_exact, n, large)


def _bias_table(dist, rb_ref, h):
    bucket = _t5_bucket(dist)
    last = rb_ref[REL_BUCKETS - 1, h]
    out = jnp.zeros(dist.shape, F32)
    for bk in range(REL_BUCKETS - 1):
        out = jnp.where(bucket == bk, rb_ref[bk, h] - last, out)
    return out


LOG2E = math.log2(math.e)
ATT_COLS = 128
I16 = jnp.int16
I16_MIN = -2 ** 15
SRCH_ROWS = 64
SRCH_W = 256
SRCH_KEYS = 1024


def _wide(x, n):
    return jnp.concatenate([x] * (n // LANES), axis=1)


def _dsa_prompt_kernel(q_ref, qi_ref, s_ref, k_ref, vt_ref, kit_ref, rb_ref, o_ref,
                       hi_s, lo_s, wb_s, bias_s, m_s, l_s, acc_s, j_s, mb_s, cand_s, thi_s, tlo_s, cnt_s,
                       *, tq, nbits, sk):
    b = pl.program_id(0)
    i = pl.program_id(1)
    row = lax.broadcasted_iota(I32, (tq, tq), 0)
    col = lax.broadcasted_iota(I32, (tq, tq), 1)

    @pl.when((b == 0) & (i == 0))
    def _():
        for h in range(C_HEADS):
            bias_s[0, h] = _bias_table(col + tq - row, rb_ref, h) * LOG2E
            bias_s[1, h] = _bias_table(col - row, rb_ref, h) * LOG2E

    @pl.when(i == 0)
    def _():
        hi_s[...] = jnp.full(hi_s.shape, I16_MIN, I16)
        lo_s[...] = jnp.full(lo_s.shape, I16_MIN, I16)

    wscale = IDX_DIM ** -0.5 * IDX_HEADS ** -0.5
    for h in range(IDX_HEADS):
        wcol = s_ref[:, IDX_DIM + h:IDX_DIM + h + 1] * wscale
        wb_s[h] = jnp.broadcast_to(wcol, (tq, LANES))

    def score_tile(j, carry):
        off = pl.multiple_of(j * tq, tq)
        kt = kit_ref[0, :, pl.ds(off, tq)]
        sc = jnp.zeros((tq, tq), F32)
        for h in range(IDX_HEADS):
            d = _dot(qi_ref[:, h * LANES:(h + 1) * LANES], kt)
            sc = sc + jnp.maximum(d, 0.0) * _wide(wb_s[h], tq)
        sc = jnp.where((j == i) & (col > row), -jnp.inf, sc)
        key = _order_key(sc)
        hi_s[:, pl.ds(off, tq)] = lax.shift_right_arithmetic(key, 16).astype(I16)
        lo_s[:, pl.ds(off, tq)] = ((key & 0xFFFF) + I16_MIN).astype(I16)
        return carry

    lax.fori_loop(0, i + 1, score_tile, 0)

    n_steps = ((i + 1) * tq + sk - 1) // sk
    ones = jnp.ones((SRCH_W, LANES), BF16)

    def scan(fn, init=None):
        groups = [slice(rc * SRCH_ROWS, (rc + 1) * SRCH_ROWS) for rc in range(tq // SRCH_ROWS)]

        def body(j, accs):
            out = []
            for rows, acc in zip(groups, accs):
                for q in range(sk // SRCH_W):
                    off = pl.multiple_of(j * sk, sk) + q * SRCH_W
                    acc = fn(rows, off, acc)
                out.append(acc)
            return tuple(out)

        outs = lax.fori_loop(0, n_steps, body, tuple(init for _ in groups))
        if init is not None:
            for rows, out in zip(groups, outs):
                cnt_s[rows, :] = out

    def count(buf, pred):
        def fn(rows, off, acc):
            return acc + jnp.where(pred(buf[rows, pl.ds(off, SRCH_W)], cand_s[rows, :]), BF16(1), BF16(0))
        scan(fn, jnp.zeros((SRCH_ROWS, SRCH_W), BF16))
        return _dot(cnt_s[...], ones)

    def search(buf, base, n_init):
        def bit_step(bi, carry):
            thr, n_ge = carry
            cand = thr + lax.shift_left(jnp.int32(1), 15 - bi)
            cand_s[...] = _wide(cand.astype(I16), SRCH_W)
            cnt = count(buf, lambda kq, c: kq >= c) + base
            ok = cnt >= TOPK
            return jnp.where(ok, cand, thr), jnp.where(ok, cnt, n_ge)
        return lax.fori_loop(0, 16, bit_step, (jnp.full((tq, LANES), I16_MIN, I32), n_init))

    n_all = jnp.zeros((tq, LANES), F32) + ((i + 1) * tq).astype(F32)
    thi, n_ge_hi = search(hi_s, 0.0, n_all)
    thi_s[...] = _wide(thi.astype(I16), SRCH_W)
    cand_s[...] = thi_s[...]
    n_gt_hi = count(hi_s, lambda kq, c: kq > c)

    def keep_tied_hi(rows, off, acc):
        tied = hi_s[rows, pl.ds(off, SRCH_W)] == thi_s[rows, :]
        lo_s[rows, pl.ds(off, SRCH_W)] = jnp.where(tied, lo_s[rows, pl.ds(off, SRCH_W)], I16(I16_MIN))
        return acc

    scan(keep_tied_hi)
    tlo, n_ge = search(lo_s, n_gt_hi, n_ge_hi)
    tlo_s[...] = _wide(tlo.astype(I16), SRCH_W)
    cand_s[...] = tlo_s[...]
    n_gt = count(lo_s, lambda kq, c: kq > c) + n_gt_hi

    need = TOPK - n_gt
    j_s[...] = jnp.full((tq, SRCH_W), 2 ** 14, I16)
    any_tie = jnp.max(jnp.where(n_ge > TOPK, 1.0, 0.0)) > 0.0

    @pl.when(any_tie)
    def _():
        lane16 = lax.broadcasted_iota(I32, (SRCH_ROWS, SRCH_W), 1)

        def tied_below(rows, off, acc):
            tied = ((hi_s[rows, pl.ds(off, SRCH_W)] == thi_s[rows, :])
                    & (lo_s[rows, pl.ds(off, SRCH_W)] == tlo_s[rows, :])
                    & ((lane16 + off).astype(I16) < cand_s[rows, :]))
            return acc + jnp.where(tied, BF16(1), BF16(0))

        def idx_step(bi, jv):
            cand = jv + lax.shift_left(jnp.int32(1), nbits - 1 - bi)
            cand_s[...] = _wide(cand.astype(I16), SRCH_W)
            scan(tied_below, jnp.zeros((SRCH_ROWS, SRCH_W), BF16))
            cnt = _dot(cnt_s[...], ones)
            return jnp.where(cnt < need, cand, jv)

        jv = lax.fori_loop(0, nbits, idx_step, jnp.zeros((tq, LANES), I32))
        j_s[...] = _wide(jv.astype(I16), SRCH_W)

    m_s[...] = jnp.full(m_s.shape, NEG_BIG, F32)
    l_s[...] = jnp.zeros(l_s.shape, F32)
    acc_s[...] = jnp.zeros(acc_s.shape, F32)
    rowg16 = (row + i * tq).astype(I16)

    def att_tile(j, tab):
        off = pl.multiple_of(j * tq, tq)
        kh16 = hi_s[:, pl.ds(off, tq)]
        kl16 = lo_s[:, pl.ds(off, tq)]
        colg16 = (col + off).astype(I16)
        thi16, tlo16 = thi_s[...], tlo_s[...]
        mask = ((kh16 > thi16) | ((kh16 == thi16) & ((kl16 > tlo16) | ((kl16 == tlo16) & (colg16 <= j_s[...])))))
        mask = mask & (colg16 <= rowg16)
        mb_s[...] = (jnp.where(mask, I16(0), I16(-1)).astype(F32) * (-NEG_BIG)).T
        for h in range(C_HEADS):
            hs = slice(h * C_HEAD_DIM, (h + 1) * C_HEAD_DIM)
            kh = k_ref[0, pl.ds(off, tq), hs]
            vth = vt_ref[0, hs, pl.ds(off, tq)]
            for qc in range(tq // ATT_COLS):
                qs = slice(qc * ATT_COLS, (qc + 1) * ATT_COLS)
                s = _dot_nt(kh, q_ref[qs, hs]) + mb_s[:, qs]
                if tab is not None:
                    s = s + bias_s[tab, h, :, qs]
                m_prev = m_s[h, :, qs]
                m_new = jnp.maximum(m_prev, jnp.max(s, axis=0, keepdims=True))
                alpha = jnp.exp2(m_prev - m_new)
                p = jnp.exp2(s - m_new)
                l_s[h, :, qs] = alpha * l_s[h, :, qs] + jnp.sum(p, axis=0, keepdims=True)
                acc_s[h, :, qs] = alpha * acc_s[h, :, qs] + _dot(vth, p.astype(BF16))
                m_s[h, :, qs] = m_new

    def far_tile(j, carry):
        att_tile(j, None)
        return carry

    lax.fori_loop(0, jnp.maximum(i - 1, 0), far_tile, 0)

    @pl.when(i >= 1)
    def _():
        att_tile(i - 1, 0)

    att_tile(i, 1)
    for h in range(C_HEADS):
        o_ref[:, h * C_HEAD_DIM:(h + 1) * C_HEAD_DIM] = (acc_s[h] / l_s[h]).T.astype(o_ref.dtype)


def dsa_prompt(q16, qi16, s32, k16, vt16, kit16, rel_bias, *, n_seq, tq=256):
    m = q16.shape[0]
    seq = m // n_seq
    nq = seq // tq
    nbits = int(math.log2(seq))
    assert 2 ** nbits == seq and seq % tq == 0
    assert tq == SRCH_W and tq >= TOPK
    kern = functools.partial(_dsa_prompt_kernel, tq=tq, nbits=nbits, sk=min(SRCH_KEYS, seq))
    once = pl.Buffered(1)
    return pl.pallas_call(
        kern,
        grid=(n_seq, nq),
        in_specs=[pl.BlockSpec((tq, C_WIDTH), lambda b, i: (b * nq + i, 0)),
                  pl.BlockSpec((tq, IDX_HEADS * LANES), lambda b, i: (b * nq + i, 0)),
                  pl.BlockSpec((tq, LANES), lambda b, i: (b * nq + i, 0)),
                  pl.BlockSpec((1, seq, C_WIDTH), lambda b, i: (b, 0, 0), pipeline_mode=once),
                  pl.BlockSpec((1, C_WIDTH, seq), lambda b, i: (b, 0, 0), pipeline_mode=once),
                  pl.BlockSpec((1, LANES, seq), lambda b, i: (b, 0, 0), pipeline_mode=once),
                  pl.BlockSpec(memory_space=pltpu.SMEM)],
        out_specs=pl.BlockSpec((tq, C_WIDTH), lambda b, i: (b * nq + i, 0)),
        out_shape=jax.ShapeDtypeStruct((m, C_WIDTH), BF16),
        scratch_shapes=[pltpu.VMEM((tq, seq), I16),
                        pltpu.VMEM((tq, seq), I16),
                        pltpu.VMEM((IDX_HEADS, tq, LANES), F32),
                        pltpu.VMEM((2, C_HEADS, tq, tq), F32),
                        pltpu.VMEM((C_HEADS, 1, tq), F32),
                        pltpu.VMEM((C_HEADS, 1, tq), F32),
                        pltpu.VMEM((C_HEADS, C_HEAD_DIM, tq), F32),
                        pltpu.VMEM((tq, SRCH_W), I16),
                        pltpu.VMEM((tq, tq), F32),
                        pltpu.VMEM((tq, SRCH_W), I16),
                        pltpu.VMEM((tq, SRCH_W), I16),
                        pltpu.VMEM((tq, SRCH_W), I16),
                        pltpu.VMEM((tq, SRCH_W), BF16)],
        compiler_params=_cparams(("arbitrary", "arbitrary")),
        name="dsa_prompt",
    )(q16, qi16, s32, k16, vt16, kit16, rel_bias)


NEW_ROWS = 16


def _dsa_sample_kernel(pt_ref, *refs, n_pages, t_new):
    ki_refs = refs[0:n_pages]
    k_refs = refs[n_pages:2 * n_pages]
    v_refs = refs[2 * n_pages:3 * n_pages]
    (qi_ref, wrep_ref, q_ref, kin_ref, knew_ref, vnew_ref, rb_ref, o_ref,
     kiall_s, xk_s, xv_s, lg_s, tab_s, exp_s) = refs[3 * n_pages:]
    b = pl.program_id(0)
    past = n_pages * PAGE
    nkeys = past + PAGE
    n_tiles = n_pages + 1
    nth = t_new * C_HEADS
    pw = PAGE * C_HEADS
    rowx = lax.broadcasted_iota(I32, (nth, pw), 0)
    lanex = lax.broadcasted_iota(I32, (nth, pw), 1)
    own_head = (lanex % C_HEADS) == (rowx % C_HEADS)

    @pl.when(b == 0)
    def _():
        kiall_s[IDX_DIM:LANES, :] = jnp.zeros((LANES - IDX_DIM, nkeys), BF16)
        xk_s[past * C_HEADS:nkeys * C_HEADS, :] = jnp.zeros((pw, C_HEAD_DIM), BF16)
        xv_s[past * C_HEADS:nkeys * C_HEADS, :] = jnp.zeros((pw, C_HEAD_DIM), BF16)
        ei = lax.broadcasted_iota(I32, (PAGE, pw), 0)
        ej = lax.broadcasted_iota(I32, (PAGE, pw), 1)
        exp_s[...] = jnp.where(ej // C_HEADS == ei, 1.0, 0.0).astype(BF16)
        t_of = rowx // C_HEADS
        k_of = lanex // C_HEADS
        tab0 = jnp.zeros((nth, pw), F32)
        tab1 = jnp.zeros((nth, pw), F32)
        for h in range(C_HEADS):
            sel = (lanex % C_HEADS) == h
            tab0 = jnp.where(sel, _bias_table(PAGE + t_of - k_of, rb_ref, h), tab0)
            tab1 = jnp.where(sel, _bias_table(t_of - k_of, rb_ref, h), tab1)
        tab_s[0] = tab0 * LOG2E
        tab_s[1] = tab1 * LOG2E

    for p in range(n_pages):
        kiall_s[0:IDX_DIM, p * PAGE:(p + 1) * PAGE] = ki_refs[p][0, 0].astype(BF16)
        xk_s[p * pw:(p + 1) * pw, :] = k_refs[p][0, 0].reshape(pw, C_HEAD_DIM).astype(BF16)
        xv_s[p * pw:(p + 1) * pw, :] = v_refs[p][0, 0].reshape(pw, C_HEAD_DIM).astype(BF16)
    kiall_s[0:IDX_DIM, past:nkeys] = kin_ref[0].astype(BF16)
    xk_s[past * C_HEADS:(past + NEW_ROWS) * C_HEADS, :] = knew_ref[0]
    xv_s[past * C_HEADS:(past + NEW_ROWS) * C_HEADS, :] = vnew_ref[0]

    d = _dot(qi_ref[0], kiall_s[...])
    r = jnp.maximum(d, 0.0) * _wide(wrep_ref[0], nkeys)
    sc = jnp.sum(r.reshape(t_new, C_HEADS, nkeys), axis=1)
    rowr = lax.broadcasted_iota(I32, (t_new, nkeys), 0)
    colr = lax.broadcasted_iota(I32, (t_new, nkeys), 1)
    valid = (colr < past) | (colr - past <= rowr)
    kr = _order_key(jnp.where(valid, sc, -jnp.inf))

    def pick(ok1, c1, ok2, c2, ok3, c3, cur):
        return jnp.where(ok3, c3, jnp.where(ok2, c2, jnp.where(ok1, c1, cur)))

    def wrap32(v):
        return ((v + 2 ** 31) % 2 ** 32) - 2 ** 31

    thr = jnp.full((t_new, 1), INT_MIN, I32)
    for bi in range(16):
        one = 1 << (30 - 2 * bi)
        c1, c2, c3 = thr + one, thr + wrap32(2 * one), thr + wrap32(3 * one)
        cnt = lambda c: jnp.sum(jnp.where(kr >= c, 1, 0), axis=1, keepdims=True)
        thr = pick(cnt(c1) >= TOPK, c1, cnt(c2) >= TOPK, c2, cnt(c3) >= TOPK, c3, thr)
    need = TOPK - jnp.sum(jnp.where(kr > thr, 1, 0), axis=1, keepdims=True)
    tie = kr == thr
    nsteps = (int(math.ceil(math.log2(nkeys))) + 1) // 2
    jv = jnp.zeros((t_new, 1), I32)
    for bi in range(nsteps):
        one = 1 << (2 * (nsteps - 1 - bi))
        c1, c2, c3 = jv + one, jv + 2 * one, jv + 3 * one
        cnt = lambda c: jnp.sum(jnp.where(tie & (colr < c), 1, 0), axis=1, keepdims=True)
        jv = pick(cnt(c1) < need, c1, cnt(c2) < need, c2, cnt(c3) < need, c3, jv)
    sel = ((kr > thr) | (tie & (colr <= jv))) & valid
    sel = jnp.broadcast_to(jnp.where(sel, 1.0, 0.0)[:, None, :], (t_new, C_HEADS, nkeys))
    sel16 = sel.reshape(nth, nkeys).astype(BF16)

    qs = q_ref[0]
    sel_rows = jnp.concatenate([sel16[:, p * PAGE:(p + 1) * PAGE] for p in range(n_tiles)], axis=0)
    selx = _dot(sel_rows, exp_s[...])
    for p in range(n_tiles):
        s = _dot_nt(qs, xk_s[p * pw:(p + 1) * pw, :])
        if p >= n_pages - 1:
            s = s + tab_s[p - (n_pages - 1)]
        lg_s[p] = jnp.where((selx[p * nth:(p + 1) * nth, :] > 0.5) & own_head, s, NEG_BIG)
    mx = jnp.full((nth, 1), NEG_BIG, F32)
    for p in range(n_tiles):
        mx = jnp.maximum(mx, jnp.max(lg_s[p], axis=1, keepdims=True))
    den = jnp.zeros((nth, 1), F32)
    out = jnp.zeros((nth, C_HEAD_DIM), F32)
    for p in range(n_tiles):
        pr = jnp.exp2(lg_s[p] - mx)
        den = den + jnp.sum(pr, axis=1, keepdims=True)
        out = out + _dot(pr.astype(BF16), xv_s[p * pw:(p + 1) * pw, :])
    o_ref[0] = out / den


def sample_dsa_operands(q16, qi16, s32, k16, v16, t_new, db):
    nth = t_new * C_HEADS
    by_seq = lambda a: a.reshape((t_new, db) + a.shape[1:]).swapaxes(0, 1)
    qi = by_seq(qi16.reshape(t_new * db, IDX_HEADS, LANES)).reshape(db, nth, LANES)
    q = by_seq(q16.reshape(t_new * db, C_HEADS, C_HEAD_DIM)).reshape(db, nth, C_HEAD_DIM)
    wscale = IDX_DIM ** -0.5 * IDX_HEADS ** -0.5
    w = by_seq(s32[:, IDX_DIM:IDX_DIM + IDX_HEADS]).reshape(db, nth, 1) * wscale
    wrep = jnp.broadcast_to(w, (db, nth, LANES))
    row_pad = ((0, 0), (0, NEW_ROWS - t_new), (0, 0))
    kin = jnp.pad(by_seq(s32[:, :IDX_DIM]).swapaxes(1, 2), ((0, 0), (0, 0), (0, PAGE - t_new)))
    new_rows = lambda a: jnp.pad(by_seq(a), row_pad).reshape(db, NEW_ROWS * C_HEADS, C_HEAD_DIM)
    return qi, wrep, q, kin, new_rows(k16), new_rows(v16)


def dsa_sample(page_table, layer, pool_ki, pool_k, pool_v, qi16, wrep, q16, kin, knew16, vnew16, rel_bias, *, t_new):
    db, n_pages = page_table.shape
    n_tiles = n_pages + 1
    nkeys = n_tiles * PAGE
    nth = t_new * C_HEADS
    pw = PAGE * C_HEADS
    kern = functools.partial(_dsa_sample_kernel, n_pages=n_pages, t_new=t_new)

    def ki_spec(p):
        return pl.BlockSpec((1, 1, IDX_DIM, PAGE), lambda b, pt: (layer, pt[b, p], 0, 0))

    def kv_spec(p):
        return pl.BlockSpec((1, 1, PAGE, C_HEADS, C_HEAD_DIM), lambda b, pt: (layer, pt[b, p], 0, 0, 0))

    per_seq = lambda shape: pl.BlockSpec((1,) + shape, lambda b, pt: (b, 0, 0))
    in_specs = ([ki_spec(p) for p in range(n_pages)]
                + [kv_spec(p) for p in range(n_pages)]
                + [kv_spec(p) for p in range(n_pages)]
                + [per_seq((nth, LANES)), per_seq((nth, LANES)), per_seq((nth, C_HEAD_DIM)),
                   per_seq((IDX_DIM, PAGE)), per_seq((NEW_ROWS * C_HEADS, C_HEAD_DIM)),
                   per_seq((NEW_ROWS * C_HEADS, C_HEAD_DIM)),
                   pl.BlockSpec(memory_space=pltpu.SMEM)])
    grid_spec = pltpu.PrefetchScalarGridSpec(
        num_scalar_prefetch=1,
        grid=(db,),
        in_specs=in_specs,
        out_specs=pl.BlockSpec((1, nth, C_HEAD_DIM), lambda b, pt: (b, 0, 0)),
        scratch_shapes=[pltpu.VMEM((LANES, nkeys), BF16),
                        pltpu.VMEM((nkeys * C_HEADS, C_HEAD_DIM), BF16),
                        pltpu.VMEM((nkeys * C_HEADS, C_HEAD_DIM), BF16),
                        pltpu.VMEM((n_tiles, nth, pw), F32),
                        pltpu.VMEM((2, nth, pw), F32),
                        pltpu.VMEM((PAGE, pw), BF16)])
    return pl.pallas_call(
        kern,
        grid_spec=grid_spec,
        out_shape=jax.ShapeDtypeStruct((db, nth, C_HEAD_DIM), F32),
        compiler_params=_cparams(("arbitrary",)),
        name="dsa_sample",
    )(page_table, *([pool_ki] * n_pages), *([pool_k] * n_pages), *([pool_v] * n_pages),
      qi16, wrep, q16, kin, knew16, vnew16, rel_bias)


PROMPT_TM = 512
FFN_TM = 512
ODD_COLS = 3 * C_WIDTH + IDX_HEADS * LANES + LANES


def _odd_weight(w):
    d = w.shape[0]
    qkv = w[:, :3 * C_WIDTH]
    qi = w[:, 3 * C_WIDTH:3 * C_WIDTH + IDX_HEADS * IDX_DIM].reshape(d, IDX_HEADS, IDX_DIM)
    qi = jnp.pad(qi, ((0, 0), (0, 0), (0, LANES - IDX_DIM))).reshape(d, IDX_HEADS * LANES)
    tail = w[:, 3 * C_WIDTH + IDX_HEADS * IDX_DIM:]
    tail = jnp.pad(tail, ((0, 0), (0, LANES - tail.shape[1])))
    return jnp.concatenate([qkv, qi, tail], axis=1).astype(BF16)


ODD_SEGS = ((0, C_WIDTH, C_HEAD_DIM ** -0.5 * LOG2E, (0,)),
            (C_WIDTH, C_WIDTH, 1.0, (1, 2)),
            (2 * C_WIDTH, C_WIDTH, 1.0, (3, 4)),
            (3 * C_WIDTH, IDX_HEADS * LANES, 1.0, (5,)),
            (3 * C_WIDTH + IDX_HEADS * LANES, LANES, 1.0, (6,)))
ODD_OUTS = ((C_WIDTH, BF16), (C_WIDTH, F32), (C_WIDTH, BF16), (C_WIDTH, F32), (C_WIDTH, BF16),
            (IDX_HEADS * LANES, BF16), (LANES, F32))
EVEN_COLS = 2 * A_WIDTH + 4 * B_WIDTH


def kernel(x_prompt, x_sample, state_conv_a, state_hgrn, cache_k, cache_v, cache_ki, state_ffn_conv, page_table, norm_mix, norm_ffn, norm_final, w_in_even, w_in_odd, w_out, conv_a_w, conv_a_b, ln_a_g, ln_a_b, hgrn_lb, hgrn_gn, rel_bias, w_up, ffn_conv_w, ffn_conv_b, w_down):
    bsz, seq, d = x_prompt.shape
    db, t_new, _ = x_sample.shape
    depth = norm_mix.shape[0]
    n_phys = cache_k.shape[1]
    to_tm = lambda a: jnp.transpose(a, (1, 0, 2))

    xp = x_prompt.reshape(bsz * seq, d)
    xs = to_tm(x_sample).reshape(t_new * db, d)
    lbs = jnp.cumsum(jax.nn.softmax(hgrn_lb.astype(F32), axis=0), axis=0)
    lbs = lbs - lbs[0:1]
    log_lb = jnp.log(lbs)
    log_1m_lb = jnp.log1p(-lbs)
    tiles_per_seq = seq // FFN_TM

    p_conv, p_hg, p_k, p_v, p_ki, p_ffn = [], [], [], [], [], []
    s_conv, s_hg, s_k, s_v, s_ki, s_ffn = [], [], [], [], [], []
    for l in range(depth):
        j = l // 2
        wo16 = w_out[l].astype(BF16)
        if l % 2 == 0:
            w16 = w_in_even[j].astype(BF16)
            segs = ((0, EVEN_COLS, 1.0, (0,)),)
            (zp,) = norm_matmul(xp, norm_mix[l], w16, segs, ((EVEN_COLS, F32),), tm=PROMPT_TM)
            (zs,) = norm_matmul(xs, norm_mix[l], w16, segs, ((EVEN_COLS, F32),), tm=PROMPT_TM)
            conv = (conv_a_w[j], conv_a_b[j], ln_a_g[j], ln_a_b[j])
            ap, cp = mixer_a(zp, *conv, jnp.zeros((bsz, A_CONV - 1, A_WIDTH), F32), n_seq=bsz, dil=1, tm=PROMPT_TM)
            bp, sp = hgrn(zp, 2, log_lb[j], log_1m_lb[j], hgrn_gn[j],
                          jnp.zeros((1, bsz, B_HEADS, B_DK, B_DK), F32), 0,
                          n_seq=bsz, rt=HG_CHUNK, n_real=HG_CHUNK, nsb=HG_CHUNK // HG_SUB)
            pre_t = to_tm(state_conv_a[j]).reshape(1, (A_CONV - 1) * db, A_WIDTH)
            a_s, cs = mixer_a(zs, *conv, pre_t, n_seq=1, dil=db, tm=t_new * db)
            zb = to_tm(zs.reshape(t_new, db, EVEN_COLS))[:, :, 2 * A_WIDTH:]
            zb = jnp.pad(zb, ((0, 0), (0, 8 - t_new), (0, 0))).reshape(db * 8, 4 * B_WIDTH)
            bs8, ss = hgrn(zb, 0, log_lb[j], log_1m_lb[j], hgrn_gn[j], state_hgrn, j,
                           n_seq=db, rt=8, n_real=t_new, nsb=1)
            b_s = to_tm(bs8.reshape(db, 8, B_WIDTH)[:, :t_new]).reshape(t_new * db, B_WIDTH)
            xp = outproj(xp, [ap, bp], [wo16[:A_WIDTH], wo16[A_WIDTH:]], tm=PROMPT_TM)
            xs = outproj(xs, [a_s, b_s], [wo16[:A_WIDTH], wo16[A_WIDTH:]], tm=PROMPT_TM)
            p_conv.append(cp)
            p_hg.append(sp)
            s_conv.append(to_tm(cs.reshape(A_CONV - 1, db, A_WIDTH)))
            s_hg.append(ss)
        else:
            w16 = _odd_weight(w_in_odd[j])
            q16, k32, k16, v32, v16, qi16, s32 = norm_matmul(xp, norm_mix[l], w16, ODD_SEGS, ODD_OUTS, tm=PROMPT_TM)
            kit16 = jnp.pad(jnp.transpose(s32[:, :IDX_DIM].reshape(bsz, seq, IDX_DIM), (0, 2, 1)),
                            ((0, 0), (0, LANES - IDX_DIM), (0, 0))).astype(BF16)
            vt16 = jnp.transpose(v16.reshape(bsz, seq, C_WIDTH), (0, 2, 1))
            mp = dsa_prompt(q16, qi16, s32, k16.reshape(bsz, seq, C_WIDTH), vt16, kit16, rel_bias, n_seq=bsz)
            p_k.append(k32.reshape(bsz, seq, C_HEADS, C_HEAD_DIM))
            p_v.append(v32.reshape(bsz, seq, C_HEADS, C_HEAD_DIM))
            p_ki.append(s32[:, :IDX_DIM].reshape(bsz, seq, IDX_DIM))
            q16s, k32s, k16s, v32s, v16s, qi16s, s32s = norm_matmul(xs, norm_mix[l], w16, ODD_SEGS, ODD_OUTS,
                                                                   tm=PROMPT_TM)
            ops = sample_dsa_operands(q16s, qi16s, s32s, k16s, v16s, t_new, db)
            o_s = dsa_sample(page_table, j, jnp.swapaxes(cache_ki, 2, 3), cache_k, cache_v, *ops, rel_bias,
                             t_new=t_new)
            m_s = to_tm(o_s.reshape(db, t_new, C_WIDTH)).reshape(t_new * db, C_WIDTH)
            xp = outproj(xp, [mp], [wo16], tm=PROMPT_TM)
            xs = outproj(xs, [m_s], [wo16], tm=PROMPT_TM)
            s_k.append(to_tm(k32s.reshape(t_new, db, C_WIDTH)).reshape(db, t_new, C_HEADS, C_HEAD_DIM))
            s_v.append(to_tm(v32s.reshape(t_new, db, C_WIDTH)).reshape(db, t_new, C_HEADS, C_HEAD_DIM))
            s_ki.append(to_tm(s32s[:, :IDX_DIM].reshape(t_new, db, IDX_DIM)))
        ffn_w = (w_up[l].astype(BF16), ffn_conv_w[l], ffn_conv_b[l], w_down[l].astype(BF16))
        xp, fcp = ffn(xp, norm_ffn[l], *ffn_w, jnp.zeros((bsz, 2, 2 * D_FF), F32), n_seq=bsz, dil=1, tm=FFN_TM)
        pre_t = to_tm(state_ffn_conv[l]).reshape(1, 2 * db, 2 * D_FF)
        xs, fcs = ffn(xs, norm_ffn[l], *ffn_w, pre_t, n_seq=1, dil=db, tm=t_new * db)
        p_ffn.append(fcp[tiles_per_seq - 1::tiles_per_seq])
        s_ffn.append(to_tm(fcs.reshape(2, db, 2 * D_FF)))
    y_prompt = rmsnorm_call(xp, norm_final, tm=PROMPT_TM).reshape(bsz, seq, d)
    y_sample = to_tm(rmsnorm_call(xs, norm_final, tm=PROMPT_TM).reshape(t_new, db, d))
    return (y_prompt, y_sample,
            jnp.stack(p_conv), jnp.stack(p_hg), jnp.stack(p_k), jnp.stack(p_v), jnp.stack(p_ki), jnp.stack(p_ffn),
            jnp.stack(s_conv), jnp.stack(s_hg), jnp.stack(s_k), jnp.stack(s_v), jnp.stack(s_ki), jnp.stack(s_ffn))
```
